```python
import math
import jax, jax.numpy as jnp
from jax import lax
import numpy as np

D_MODEL = 1024
BATCH = 8
SEQ = 4096
DEPTH = 2

HEAD_DIM = 64
GRID_W = 64
ROPE_THETA = 10000.0
NORM_EPS = 1e-6
NEG_INF = -1e30

A_Q_HEADS = 8
A_KV_HEADS = 2
A_RADIUS = 128
B_PAIRS = ((128, 1), (512, 4), (2048, 16))
B_HEADS_PER_GROUP = 2
B_HEADS = B_HEADS_PER_GROUP * len(B_PAIRS)
C_HEADS = 4
C_QK_DIM = 32
C_V_DIM = 2 * C_QK_DIM
C_Q_BLOCK = 128
D_HEADS = 4
NA_ROWS = 8
NA_COLS = 16
D_MLP = 4 * D_MODEL
N_BRANCHES = 4

A_Q_COLS = A_Q_HEADS * HEAD_DIM
A_KV_COLS = A_KV_HEADS * HEAD_DIM
B_COLS = B_HEADS * HEAD_DIM
C_QK_COLS = C_HEADS * 2 * C_QK_DIM
C_V_COLS = C_HEADS * C_V_DIM
D_COLS = D_HEADS * HEAD_DIM
GATE_COLS = N_BRANCHES * D_MODEL
IN_SPLITS = (A_Q_COLS, A_KV_COLS, A_KV_COLS, B_COLS, B_COLS, B_COLS,
             C_QK_COLS, C_QK_COLS, C_V_COLS, D_COLS, D_COLS, D_COLS, GATE_COLS)
IN_COLS = sum(IN_SPLITS)
A_OUT = A_Q_HEADS * HEAD_DIM
B_OUT = B_HEADS_PER_GROUP * HEAD_DIM
C_OUT = C_HEADS * C_V_DIM
D_OUT = D_HEADS * HEAD_DIM

kernel_name = "hybrid_gated_parallel_attention_encoder"


def rms_norm(x, g):
    xf = x.astype(jnp.float32)
    y = xf * lax.rsqrt(jnp.mean(xf * xf, axis=-1, keepdims=True) + NORM_EPS)
    return (y * g.astype(jnp.float32)).astype(x.dtype)


def rope_tables(n, dim):
    inv = ROPE_THETA ** (-jnp.arange(0, dim, 2, dtype=jnp.float32) / dim)
    ang = jnp.arange(n, dtype=jnp.float32)[:, None] * inv[None, :]
    return jnp.cos(ang), jnp.sin(ang)


def apply_rope(x, cos, sin):
    half = x.shape[-1] // 2
    x1, x2 = x[..., :half], x[..., half:]
    c, s = cos.astype(x.dtype), sin.astype(x.dtype)
    return jnp.concatenate([x1 * c - x2 * s, x2 * c + x1 * s], axis=-1)


def split_heads(t, n):
    b, s, _ = t.shape
    return t.reshape(b, s, n, -1).transpose(0, 2, 1, 3)


def merge_heads(t):
    b, h, s, d = t.shape
    return t.transpose(0, 2, 1, 3).reshape(b, s, h * d)


def banded_attention(q, k, v, radius, sink=None):
    b, g, r, l, dh = q.shape
    dv = v.shape[-1]
    blk = radius
    nb = -(-l // blk)
    lp = nb * blk
    pad = lp - l
    qb = jnp.pad(q, ((0, 0), (0, 0), (0, 0), (0, pad), (0, 0))).reshape(b, g, r, nb, blk, dh)
    kp = jnp.pad(k, ((0, 0), (0, 0), (blk, pad + blk), (0, 0)))
    vp = jnp.pad(v, ((0, 0), (0, 0), (blk, pad + blk), (0, 0)))
    kb = jnp.concatenate([kp[:, :, o * blk:o * blk + lp].reshape(b, g, nb, blk, dh) for o in range(3)], axis=3)
    vb = jnp.concatenate([vp[:, :, o * blk:o * blk + lp].reshape(b, g, nb, blk, dv) for o in range(3)], axis=3)
    qpos = jnp.arange(lp).reshape(nb, blk)
    kpos = jnp.arange(nb)[:, None] * blk - blk + jnp.arange(3 * blk)[None, :]
    valid = ((kpos[:, None, :] >= 0) & (kpos[:, None, :] < l)
             & (jnp.abs(kpos[:, None, :] - qpos[:, :, None]) <= radius))
    s = jnp.einsum('bgrnqd,bgnkd->bgrnqk', qb, kb).astype(jnp.float32) * (dh ** -0.5)
    s = jnp.where(valid, s, NEG_INF)
    m = jnp.max(s, axis=-1, keepdims=True)
    if sink is not None:
        sk = sink.astype(jnp.float32)[None, :, :, None, None, None]
        m = jnp.maximum(m, sk)
        p = jnp.exp(s - m)
        den = jnp.sum(p, axis=-1, keepdims=True) + jnp.exp(sk - m)
    else:
        p = jnp.exp(s - m)
        den = jnp.sum(p, axis=-1, keepdims=True)
    out = jnp.einsum('bgrnqk,bgnkd->bgrnqd', (p / den).astype(v.dtype), vb)
    lse = (jnp.log(den) + m)[..., 0]
    out = out.reshape(b, g, r, lp, dv)[:, :, :, :l]
    lse = lse.reshape(b, g, r, lp)[:, :, :, :l]
    return out, lse


def mixer_window_gqa(q, k, v, qn_g, kn_g, sink, cos, sin):
    b, s, _ = q.shape
    q = apply_rope(rms_norm(split_heads(q, A_Q_HEADS), qn_g), cos, sin)
    k = apply_rope(rms_norm(split_heads(k, A_KV_HEADS), kn_g), cos, sin)
    v = split_heads(v, A_KV_HEADS)
    rep = A_Q_HEADS // A_KV_HEADS
    q = q.reshape(b, A_KV_HEADS, rep, s, HEAD_DIM)
    out, _ = banded_attention(q, k, v, A_RADIUS, sink.reshape(A_KV_HEADS, rep))
    return merge_heads(out.reshape(b, A_Q_HEADS, s, HEAD_DIM))


def to_residue_classes(t, d):
    b, h, s, dh = t.shape
    return t.reshape(b, h, s // d, d, dh).transpose(0, 1, 3, 2, 4).reshape(b, h * d, s // d, dh)


def from_residue_classes(t, h, d):
    b, _, sd = t.shape[:3]
    rest = t.shape[3:]
    t = t.reshape((b, h, d, sd) + rest)
    perm = (0, 1, 3, 2) + tuple(range(4, t.ndim))
    return t.transpose(perm).reshape((b, h, sd * d) + rest)


def mixer_dilated(q, k, v, qn_g, kn_g, cos, sin):
    q = apply_rope(rms_norm(split_heads(q, B_HEADS), qn_g), cos, sin)
    k = apply_rope(rms_norm(split_heads(k, B_HEADS), kn_g), cos, sin)
    v = split_heads(v, B_HEADS)
    hpg = B_HEADS_PER_GROUP
    outs, lses = [], []
    for gi, (window, dil) in enumerate(B_PAIRS):
        hs = slice(gi * hpg, (gi + 1) * hpg)
        qd = to_residue_classes(q[:, hs], dil)[:, :, None]
        kd = to_residue_classes(k[:, hs], dil)
        vd = to_residue_classes(v[:, hs], dil)
        o, lse = banded_attention(qd, kd, vd, window // (2 * dil))
        outs.append(from_residue_classes(o[:, :, 0], hpg, dil))
        lses.append(from_residue_classes(lse[:, :, 0], hpg, dil))
    wts = jax.nn.softmax(jnp.stack(lses, axis=0), axis=0)
    out = jnp.sum(wts[..., None] * jnp.stack(outs, axis=0).astype(jnp.float32), axis=0)
    return merge_heads(out.astype(q.dtype))


def mixer_differential(q, k, v, qn_g, kn_g, lam_p, subln_g, cos, sin, lambda_init):
    b, s, _ = q.shape
    q = q.reshape(b, s, C_HEADS, 2, C_QK_DIM).transpose(0, 2, 3, 1, 4)
    k = k.reshape(b, s, C_HEADS, 2, C_QK_DIM).transpose(0, 2, 3, 1, 4)
    q = apply_rope(rms_norm(q, qn_g), cos, sin)
    k = apply_rope(rms_norm(k, kn_g), cos, sin)
    v = split_heads(v, C_HEADS)
    lp = lam_p.astype(jnp.float32)
    lam = jnp.exp(jnp.sum(lp[0] * lp[1])) - jnp.exp(jnp.sum(lp[2] * lp[3])) + lambda_init
    nq = s // C_Q_BLOCK
    qb = q.reshape(b, C_HEADS, 2, nq, C_Q_BLOCK, C_QK_DIM).transpose(3, 0, 1, 2, 4, 5)
    scale = C_QK_DIM ** -0.5

    def block(qi):
        sc = jnp.einsum('bhcqd,bhckd->bhcqk', qi, k).astype(jnp.float32) * scale
        p = jax.nn.softmax(sc, axis=-1)
        a = p[:, :, 0] - lam * p[:, :, 1]
        return jnp.einsum('bhqk,bhkd->bhqd', a.astype(v.dtype), v)

    o = lax.map(block, qb)
    o = o.transpose(1, 2, 0, 3, 4).reshape(b, C_HEADS, s, C_V_DIM)
    o = rms_norm(o, subln_g) * (1.0 - lambda_init)
    return merge_heads(o)


def mixer_neighbourhood(q, k, v, qn_g, kn_g, rpb):
    b, s, _ = q.shape
    rows = s // GRID_W
    kr = min(NA_ROWS, rows)
    q = rms_norm(split_heads(q, D_HEADS), qn_g).reshape(b, D_HEADS, rows, GRID_W, HEAD_DIM)
    k = rms_norm(split_heads(k, D_HEADS), kn_g).reshape(b, D_HEADS, rows, GRID_W, HEAD_DIM)
    v = split_heads(v, D_HEADS).reshape(b, D_HEADS, rows, GRID_W, HEAD_DIM)
    qi = jnp.arange(rows)
    r0 = jnp.clip(qi - kr // 2, 0, rows - kr)
    ridx = r0[:, None] + jnp.arange(kr)[None, :]
    cj = jnp.arange(GRID_W)
    c0 = jnp.clip(cj - NA_COLS // 2, 0, GRID_W - NA_COLS)
    col_ok = (cj[None, :] >= c0[:, None]) & (cj[None, :] < c0[:, None] + NA_COLS)
    mask = jnp.tile(col_ok, (1, kr))
    kg = k[:, :, ridx].reshape(b, D_HEADS, rows, kr * GRID_W, HEAD_DIM)
    vg = v[:, :, ridx].reshape(b, D_HEADS, rows, kr * GRID_W, HEAD_DIM)
    dr = ridx - qi[:, None] + (NA_ROWS - 1)
    dc = jnp.clip(cj[None, :] - cj[:, None], -(NA_COLS - 1), NA_COLS - 1) + (NA_COLS - 1)
    bias = rpb[:, dr[:, None, :, None], dc[None, :, None, :]]
    bias = bias.reshape(D_HEADS, rows, GRID_W, kr * GRID_W).astype(jnp.float32)
    sc = jnp.einsum('bhiqd,bhikd->bhiqk', q, kg).astype(jnp.float32) * (HEAD_DIM ** -0.5) + bias
    sc = jnp.where(mask, sc, NEG_INF)
    p = jax.nn.softmax(sc, axis=-1)
    o = jnp.einsum('bhiqk,bhikd->bhiqd', p.astype(v.dtype), vg)
    return merge_heads(o.reshape(b, D_HEADS, s, HEAD_DIM))


def setup_inputs(seed: int = 0) -> dict:
    key = jax.random.key(seed)
    ks = jax.random.split(key, 19)
    f32 = jnp.float32
    nrm = lambda kk, shape, sc: jax.random.normal(kk, shape, f32) * sc
    gain = lambda kk, shape: 1.0 + 0.02 * jax.random.normal(kk, shape, f32)
    return {
        "x": nrm(ks[0], (BATCH, SEQ, D_MODEL), 1.0),
        "attn_norm_g": gain(ks[1], (DEPTH, D_MODEL)),
        "w_in": nrm(ks[2], (DEPTH, D_MODEL, IN_COLS), D_MODEL ** -0.5),
        "a_qk_norm_g": gain(ks[3], (DEPTH, 2, HEAD_DIM)),
        "a_sink": nrm(ks[4], (DEPTH, A_Q_HEADS), 0.5),
        "b_qk_norm_g": gain(ks[5], (DEPTH, 2, HEAD_DIM)),
        "c_qk_norm_g": gain(ks[6], (DEPTH, 2, C_QK_DIM)),
        "c_lambda": nrm(ks[7], (DEPTH, 4, C_QK_DIM), 0.1),
        "c_subln_g": gain(ks[8], (DEPTH, C_V_DIM)),
        "d_qk_norm_g": gain(ks[9], (DEPTH, 2, HEAD_DIM)),
        "d_rel_bias": nrm(ks[10], (DEPTH, D_HEADS, 2 * NA_ROWS - 1, 2 * NA_COLS - 1), 0.1),
        "w_branch_a": nrm(ks[11], (DEPTH, A_OUT, D_MODEL), A_OUT ** -0.5),
        "w_branch_b": nrm(ks[12], (DEPTH, B_OUT, D_MODEL), B_OUT ** -0.5),
        "w_branch_c": nrm(ks[13], (DEPTH, C_OUT, D_MODEL), C_OUT ** -0.5),
        "w_branch_d": nrm(ks[14], (DEPTH, D_OUT, D_MODEL), D_OUT ** -0.5),
        "w_out": nrm(ks[15], (DEPTH, D_MODEL, D_MODEL), D_MODEL ** -0.5),
        "mlp_norm_g": gain(ks[16], (DEPTH, D_MODEL)),
        "w_up": nrm(ks[17], (DEPTH, D_MODEL, D_MLP), D_MODEL ** -0.5),
        "w_down": nrm(ks[18], (DEPTH, D_MLP, D_MODEL), D_MLP ** -0.5),
    }


def reference(x, attn_norm_g, w_in, a_qk_norm_g, a_sink, b_qk_norm_g, c_qk_norm_g, c_lambda,
              c_subln_g, d_qk_norm_g, d_rel_bias, w_branch_a, w_branch_b, w_branch_c, w_branch_d,
              w_out, mlp_norm_g, w_up, w_down):
    b, s, _ = x.shape
    cos64, sin64 = rope_tables(s, HEAD_DIM)
    cos32, sin32 = rope_tables(s, C_QK_DIM)
    offsets = []
    acc = 0
    for w in IN_SPLITS[:-1]:
        acc += w
        offsets.append(acc)
    for l in range(DEPTH):
        h = rms_norm(x, attn_norm_g[l])
        proj = jnp.einsum('bsd,de->bse', h, w_in[l])
        aq, ak, av, bq, bk, bv, cq, ck, cv, dq, dk, dv, gate = jnp.split(proj, offsets, axis=-1)
        ya = mixer_window_gqa(aq, ak, av, a_qk_norm_g[l, 0], a_qk_norm_g[l, 1], a_sink[l], cos64, sin64)
        yb = mixer_dilated(bq, bk, bv, b_qk_norm_g[l, 0], b_qk_norm_g[l, 1], cos64, sin64)
        lambda_init = 0.8 - 0.6 * math.exp(-0.3 * l)
        yc = mixer_differential(cq, ck, cv, c_qk_norm_g[l, 0], c_qk_norm_g[l, 1], c_lambda[l],
                                c_subln_g[l], cos32, sin32, lambda_init)
        yd = mixer_neighbourhood(dq, dk, dv, d_qk_norm_g[l, 0], d_qk_norm_g[l, 1], d_rel_bias[l])
        g = jax.nn.sigmoid(gate.astype(jnp.float32)).astype(x.dtype).reshape(b, s, N_BRANCHES, D_MODEL)
        merged = (g[:, :, 0] * (ya @ w_branch_a[l]) + g[:, :, 1] * (yb @ w_branch_b[l])
                  + g[:, :, 2] * (yc @ w_branch_c[l]) + g[:, :, 3] * (yd @ w_branch_d[l]))
        x = x + merged @ w_out[l]
        hm = rms_norm(x, mlp_norm_g[l])
        u = jnp.square(jax.nn.relu(hm @ w_up[l]))
        x = x + u @ w_down[l]
    return x
```

```python
import functools
import math

import jax
import jax.numpy as jnp
import numpy as np
from jax import lax
from jax.experimental import pallas as pl
from jax.experimental.pallas import tpu as pltpu

F32 = jnp.float32
BF16 = jnp.bfloat16

D_MODEL = 1024
HEAD_DIM = 64
GRID_W = 64
ROPE_THETA = 10000.0
NORM_EPS = 1e-6
NEG_INF = -1e30

A_Q_HEADS = 8
A_KV_HEADS = 2
A_RADIUS = 128
B_PAIRS = ((128, 1), (512, 4), (2048, 16))
B_HEADS_PER_GROUP = 2
B_HEADS = B_HEADS_PER_GROUP * len(B_PAIRS)
C_HEADS = 4
C_QK_DIM = 32
C_V_DIM = 2 * C_QK_DIM
D_HEADS = 4
NA_ROWS = 8
NA_COLS = 16
D_MLP = 4 * D_MODEL
N_BRANCHES = 4

A_Q_COLS = A_Q_HEADS * HEAD_DIM
A_KV_COLS = A_KV_HEADS * HEAD_DIM
B_COLS = B_HEADS * HEAD_DIM
C_QK_COLS = C_HEADS * 2 * C_QK_DIM
C_V_COLS = C_HEADS * C_V_DIM
D_COLS = D_HEADS * HEAD_DIM
GATE_COLS = N_BRANCHES * D_MODEL
QKV_COLS = (A_Q_COLS + 2 * A_KV_COLS + 3 * B_COLS + 2 * C_QK_COLS + C_V_COLS + 3 * D_COLS)

LANES = 128
VMEM_LIMIT = 48 * 1024 * 1024

_SECTIONS = (
    ("aq", A_Q_COLS, HEAD_DIM, True), ("ak", A_KV_COLS, HEAD_DIM, True), ("av", A_KV_COLS, None, False),
    ("bq", B_COLS, HEAD_DIM, True), ("bk", B_COLS, HEAD_DIM, True), ("bv", B_COLS, None, False),
    ("cq", C_QK_COLS, C_QK_DIM, True), ("ck", C_QK_COLS, C_QK_DIM, True), ("cv", C_V_COLS, None, False),
    ("dq", D_COLS, HEAD_DIM, False), ("dk", D_COLS, HEAD_DIM, False), ("dv", D_COLS, None, False),
)


def _nt_dot(a, b):
    return lax.dot_general(a, b, (((1,), (1,)), ((), ())), preferred_element_type=F32)


def _dot(a, b):
    return jnp.dot(a, b, preferred_element_type=F32)


def _segment_sumsq(t, ones_blockdiag):
    t2 = t * t
    hi = t2.astype(BF16)
    lo = (t2 - hi.astype(F32)).astype(BF16)
    return _dot(hi, ones_blockdiag) + _dot(lo, ones_blockdiag)


def _rms_rows(x, g):
    ms = jnp.mean(x * x, axis=-1, keepdims=True)
    return x * lax.rsqrt(ms + NORM_EPS) * g


def _rotate_half(y, half):
    lane = lax.broadcasted_iota(jnp.int32, y.shape, 1)
    first = (lane & (2 * half - 1)) < half
    return jnp.where(first, pltpu.roll(y, LANES - half, 1), pltpu.roll(y, half, 1))


def _proj_kernel(x_ref, g_ref, wqkv_ref, wg_ref, gains_ref, cos64_ref, sin64_ref, cos32_ref, sin32_ref,
                 e64_ref, e32_ref, *out_refs):
    x = x_ref[...]
    h = _rms_rows(x, g_ref[...]).astype(BF16)
    col = 0
    for idx, (_, width, hd, rope) in enumerate(_SECTIONS):
        t = _dot(h, wqkv_ref[:, col:col + width])
        o_ref = out_refs[idx]
        if hd is None:
            o_ref[...] = t.astype(BF16)
        else:
            e_ref = e64_ref if hd == HEAD_DIM else e32_ref
            for c in range(width // LANES):
                tc = t[:, c * LANES:(c + 1) * LANES]
                ss = _segment_sumsq(tc, e_ref[...])
                gain = gains_ref[:, col + c * LANES:col + (c + 1) * LANES]
                y = tc * lax.rsqrt(ss * (1.0 / hd) + NORM_EPS) * gain
                if rope:
                    cos_ref, sin_ref = (cos64_ref, sin64_ref) if hd == HEAD_DIM else (cos32_ref, sin32_ref)
                    y = y * cos_ref[...] + _rotate_half(y, hd // 2) * sin_ref[...]
                o_ref[:, c * LANES:(c + 1) * LANES] = y.astype(BF16)
        col += width
    gate_ref = out_refs[len(_SECTIONS)]
    for c in range(N_BRANCHES):
        z = _dot(h, wg_ref[:, c * D_MODEL:(c + 1) * D_MODEL])
        gate_ref[:, c * D_MODEL:(c + 1) * D_MODEL] = (1.0 / (1.0 + jnp.exp(-z))).astype(BF16)


def _proj_call(x2d, g, wqkv, wg, gains, tabs, seq, tm):
    t_tokens = x2d.shape[0]
    n_seq_tiles = seq // tm
    const = lambda i: (0, 0)
    row = lambda i: (i, 0)
    pos = lambda i: (i % n_seq_tiles, 0)
    single = pl.Buffered(1)
    in_specs = [
        pl.BlockSpec((tm, D_MODEL), row),
        pl.BlockSpec((1, D_MODEL), const),
        pl.BlockSpec((D_MODEL, QKV_COLS), const, pipeline_mode=single),
        pl.BlockSpec((D_MODEL, GATE_COLS), const, pipeline_mode=single),
        pl.BlockSpec((1, QKV_COLS), const),
        pl.BlockSpec((tm, LANES), pos), pl.BlockSpec((tm, LANES), pos),
        pl.BlockSpec((tm, LANES), pos), pl.BlockSpec((tm, LANES), pos),
        pl.BlockSpec((LANES, LANES), const), pl.BlockSpec((LANES, LANES), const),
    ]
    widths = [s[1] for s in _SECTIONS] + [GATE_COLS]
    out_shape = [jax.ShapeDtypeStruct((t_tokens, w), BF16) for w in widths]
    out_specs = [pl.BlockSpec((tm, w), row) for w in widths]
    return pl.pallas_call(
        _proj_kernel, grid=(t_tokens // tm,), in_specs=in_specs, out_specs=out_specs, out_shape=out_shape,
        compiler_params=pltpu.CompilerParams(dimension_semantics=("arbitrary",), vmem_limit_bytes=VMEM_LIMIT),
        name="proj",
    )(x2d, g, wqkv, wg, gains, tabs["cos64"], tabs["sin64"], tabs["cos32"], tabs["sin32"], tabs["e64"], tabs["e32"])


def _band_bias(n, tq, win, radius, start):
    qpos = n * tq + lax.broadcasted_iota(jnp.int32, (tq, win), 0)
    kpos = start + lax.broadcasted_iota(jnp.int32, (tq, win), 1)
    return jnp.where(jnp.abs(kpos - qpos) <= radius, 0.0, NEG_INF).astype(F32)


def _window_kernel(sink_ref, q_ref, k_ref, v_ref, o_ref, *, tq, seq, radius):
    n = pl.program_id(1)
    win = tq + 2 * radius
    start = pl.multiple_of(jnp.clip(n * tq - radius, 0, seq - win), radius)
    q = q_ref[0]
    k = k_ref[0, pl.ds(start, win), :]
    v = v_ref[0, pl.ds(start, win), :]
    bias = _band_bias(n, tq, win, radius, start)
    rep = A_Q_HEADS // A_KV_HEADS
    outs = []
    for hq in range(A_Q_HEADS):
        g = hq // rep
        qh = q[:, hq * HEAD_DIM:(hq + 1) * HEAD_DIM]
        kh = k[:, g * HEAD_DIM:(g + 1) * HEAD_DIM]
        vh = v[:, g * HEAD_DIM:(g + 1) * HEAD_DIM]
        s = _nt_dot(qh, kh) + bias
        sk = sink_ref[hq]
        m = jnp.maximum(jnp.max(s, axis=-1, keepdims=True), sk)
        p = jnp.exp(s - m)
        den = jnp.sum(p, axis=-1, keepdims=True) + jnp.exp(sk - m)
        outs.append(_dot(p.astype(BF16), vh) / den)
    o_ref[0] = jnp.concatenate(outs, axis=-1).astype(BF16)


def _window_call(sink, q, k, v, tq):
    b, seq, _ = q.shape
    kern = functools.partial(_window_kernel, tq=tq, seq=seq, radius=A_RADIUS)
    return pl.pallas_call(
        kern, grid=(b, seq // tq),
        in_specs=[pl.BlockSpec(memory_space=pltpu.SMEM),
                  pl.BlockSpec((1, tq, A_Q_COLS), lambda i, n: (i, n, 0)),
                  pl.BlockSpec((1, seq, A_KV_COLS), lambda i, n: (i, 0, 0)),
                  pl.BlockSpec((1, seq, A_KV_COLS), lambda i, n: (i, 0, 0))],
        out_specs=pl.BlockSpec((1, tq, A_Q_COLS), lambda i, n: (i, n, 0)),
        out_shape=jax.ShapeDtypeStruct((b, seq, A_Q_COLS), BF16),
        compiler_params=pltpu.CompilerParams(dimension_semantics=("arbitrary", "arbitrary")),
        name="mixer_a",
    )(sink, q, k, v)


def _dilated_kernel(q_ref, k_ref, v_ref, o_ref, lse_ref, *, tq, sub_len, radius):
    n = pl.program_id(2)
    win = tq + 2 * radius
    start = pl.multiple_of(jnp.clip(n * tq - radius, 0, sub_len - win), radius)
    q = q_ref[0]
    k = k_ref[0, pl.ds(start, win), :]
    v = v_ref[0, pl.ds(start, win), :]
    bias = _band_bias(n, tq, win, radius, start)
    outs, lses = [], []
    for hh in range(B_HEADS_PER_GROUP):
        sl = slice(hh * HEAD_DIM, (hh + 1) * HEAD_DIM)
        s = _nt_dot(q[:, sl], k[:, sl]) + bias
        m = jnp.max(s, axis=-1, keepdims=True)
        p = jnp.exp(s - m)
        den = jnp.sum(p, axis=-1, keepdims=True)
        outs.append(_dot(p.astype(BF16), v[:, sl]) / den)
        lses.append(jnp.broadcast_to(jnp.log(den) + m, (tq, HEAD_DIM)))
    o_ref[0] = jnp.concatenate(outs, axis=-1)
    lse_ref[0] = jnp.concatenate(lses, axis=-1)


def _dilated_call(q, k, v, group, window, dil, tq):
    b, seq, _ = q.shape
    sub_len = seq // dil
    radius = window // (2 * dil)
    chunks = B_COLS // LANES
    qv, kv, vv = (t.reshape(b, sub_len, dil * B_COLS) for t in (q, k, v))
    kern = functools.partial(_dilated_kernel, tq=tq, sub_len=sub_len, radius=radius)
    out_sds = jax.ShapeDtypeStruct((b, sub_len, dil * LANES), F32)
    o, lse = pl.pallas_call(
        kern, grid=(b, dil, sub_len // tq),
        in_specs=[pl.BlockSpec((1, tq, LANES), lambda i, r, n: (i, n, r * chunks + group)),
                  pl.BlockSpec((1, sub_len, LANES), lambda i, r, n: (i, 0, r * chunks + group)),
                  pl.BlockSpec((1, sub_len, LANES), lambda i, r, n: (i, 0, r * chunks + group))],
        out_specs=[pl.BlockSpec((1, tq, LANES), lambda i, r, n: (i, n, r)),
                   pl.BlockSpec((1, tq, LANES), lambda i, r, n: (i, n, r))],
        out_shape=[out_sds, out_sds],
        compiler_params=pltpu.CompilerParams(dimension_semantics=("arbitrary",) * 3),
        name=f"mixer_b_d{dil}",
    )(qv, kv, vv)
    return o.reshape(b, seq, LANES), lse.reshape(b, seq, LANES)


def _diff_kernel(q_ref, k_ref, v_ref, lam_ref, subg_ref, e64_ref, o_ref, vext_ref, acc_ref, m_ref,
                 *, tq, tk, seq, lambda_init):
    qi = pl.program_id(2)
    nseg = LANES // C_QK_DIM

    @pl.when(qi == 0)
    def _():
        vext_ref[:, :LANES] = v_ref[0]
        vext_ref[:, LANES:] = jnp.ones((seq, LANES), BF16)

    q = q_ref[0]
    lane = lax.broadcasted_iota(jnp.int32, (tq, LANES), 1)
    zero = jnp.zeros_like(q)
    q4 = jnp.concatenate([jnp.where((lane // C_QK_DIM) == j, q, zero) for j in range(nseg)], axis=0)
    m_ref[...] = jnp.full(m_ref.shape, NEG_INF, F32)
    acc_ref[...] = jnp.zeros(acc_ref.shape, F32)

    def body(t, carry):
        off = pl.multiple_of(t * tk, tk)
        kb = k_ref[0, pl.ds(off, tk), :]
        s = _nt_dot(q4, kb)
        m_prev = m_ref[...]
        m_new = jnp.maximum(m_prev, jnp.max(s, axis=-1, keepdims=True))
        alpha = jnp.exp(m_prev - m_new)
        p = jnp.exp(s - m_new).astype(BF16)
        acc_ref[...] = acc_ref[...] * alpha + _dot(p, vext_ref[pl.ds(off, tk), :])
        m_ref[...] = m_new
        return carry

    lax.fori_loop(0, seq // tk, body, 0)

    lp = lam_ref[...]
    lam = (jnp.exp(jnp.sum(lp[0:1] * lp[1:2], axis=-1, keepdims=True))
           - jnp.exp(jnp.sum(lp[2:3] * lp[3:4], axis=-1, keepdims=True)) + lambda_init)
    acc = acc_ref[...]
    u = [acc[j * tq:(j + 1) * tq, :LANES] / acc[j * tq:(j + 1) * tq, LANES:] for j in range(nseg)]
    o = jnp.where(lane < C_V_DIM, u[0] - lam * u[1], u[2] - lam * u[3])
    ss = _segment_sumsq(o, e64_ref[...])
    o = o * lax.rsqrt(ss * (1.0 / C_V_DIM) + NORM_EPS) * subg_ref[...] * (1.0 - lambda_init)
    o_ref[0] = o.astype(BF16)


def _diff_call(q, k, v, lam_p, subg, e64, lambda_init, tq, tk):
    b, seq, _ = q.shape
    pairs = C_QK_COLS // LANES
    nseg = LANES // C_QK_DIM
    kern = functools.partial(_diff_kernel, tq=tq, tk=tk, seq=seq, lambda_init=lambda_init)
    return pl.pallas_call(
        kern, grid=(b, pairs, seq // tq),
        in_specs=[pl.BlockSpec((1, tq, LANES), lambda i, p, n: (i, n, p)),
                  pl.BlockSpec((1, seq, LANES), lambda i, p, n: (i, 0, p)),
                  pl.BlockSpec((1, seq, LANES), lambda i, p, n: (i, 0, p)),
                  pl.BlockSpec((4, C_QK_DIM), lambda i, p, n: (0, 0)),
                  pl.BlockSpec((1, LANES), lambda i, p, n: (0, 0)),
                  pl.BlockSpec((LANES, LANES), lambda i, p, n: (0, 0))],
        out_specs=pl.BlockSpec((1, tq, LANES), lambda i, p, n: (i, n, p)),
        out_shape=jax.ShapeDtypeStruct((b, seq, C_V_COLS), BF16),
        scratch_shapes=[pltpu.VMEM((seq, 2 * LANES), BF16),
                        pltpu.VMEM((nseg * tq, 2 * LANES), F32),
                        pltpu.VMEM((nseg * tq, 1), F32)],
        compiler_params=pltpu.CompilerParams(dimension_semantics=("arbitrary",) * 3),
        name="mixer_c",
    )(q, k, v, lam_p, subg, e64)


def _nbr_kernel(q_ref, k_ref, v_ref, bias_ref, o_ref, *, rows):
    kr = NA_ROWS
    lane = lax.broadcasted_iota(jnp.int32, (GRID_W, LANES), 1)
    left = lane < HEAD_DIM

    def body(i, carry):
        r0 = jnp.clip(i - kr // 2, 0, rows - kr)
        cls = jnp.minimum(i, kr // 2) + jnp.maximum(i - (rows - kr // 2), 0)
        q = q_ref[0, pl.ds(pl.multiple_of(i * GRID_W, GRID_W), GRID_W), :]
        koff = pl.multiple_of(r0 * GRID_W, GRID_W)
        k = k_ref[0, pl.ds(koff, kr * GRID_W), :]
        v = v_ref[0, pl.ds(koff, kr * GRID_W), :]
        zero = jnp.zeros_like(q)
        q2 = jnp.concatenate([jnp.where(left, q, zero), jnp.where(left, zero, q)], axis=0)
        s = _nt_dot(q2, k) + bias_ref[cls].reshape(2 * GRID_W, kr * GRID_W)
        m = jnp.max(s, axis=-1, keepdims=True)
        p = jnp.exp(s - m)
        den = jnp.sum(p, axis=-1, keepdims=True)
        pv = _dot(p.astype(BF16), v) / den
        o = jnp.where(left, pv[:GRID_W], pv[GRID_W:])
        o_ref[0, pl.ds(pl.multiple_of(i * GRID_W, GRID_W), GRID_W), :] = o.astype(BF16)
        return carry

    lax.fori_loop(0, rows, body, 0)


def _nbr_call(q, k, v, bias_tab):
    b, seq, _ = q.shape
    rows = seq // GRID_W
    pairs = D_COLS // LANES
    n_cls = bias_tab.shape[0]
    kern = functools.partial(_nbr_kernel, rows=rows)
    blk = lambda i, p: (i, 0, p)
    return pl.pallas_call(
        kern, grid=(b, pairs),
        in_specs=[pl.BlockSpec((1, seq, LANES), blk), pl.BlockSpec((1, seq, LANES), blk),
                  pl.BlockSpec((1, seq, LANES), blk),
                  pl.BlockSpec((n_cls, 2, GRID_W, NA_ROWS * GRID_W), lambda i, p: (0, p, 0, 0))],
        out_specs=pl.BlockSpec((1, seq, LANES), blk),
        out_shape=jax.ShapeDtypeStruct((b, seq, D_COLS), BF16),
        compiler_params=pltpu.CompilerParams(dimension_semantics=("arbitrary",) * 2),
        name="mixer_d",
    )(q, k, v, bias_tab)


def _nbr_bias_table(rpb, rows):
    kr = NA_ROWS
    n_cls = kr
    cj = np.arange(GRID_W)
    c0 = np.clip(cj - NA_COLS // 2, 0, GRID_W - NA_COLS)
    col_ok = (cj[None, :] >= c0[:, None]) & (cj[None, :] < c0[:, None] + NA_COLS)
    dc = np.clip(cj[None, :] - cj[:, None], -(NA_COLS - 1), NA_COLS - 1) + (NA_COLS - 1)
    dr = np.arange(kr)[None, :] - np.arange(n_cls)[:, None] + (NA_ROWS - 1)
    tab = rpb[:, dr[:, None, :, None], dc[None, :, None, :]]
    tab = jnp.where(col_ok[None, None, :, None, :], tab.astype(F32), NEG_INF)
    return tab.transpose(1, 0, 2, 3, 4).reshape(n_cls, D_HEADS, GRID_W, kr * GRID_W)


def _merge_kernel(x_ref, gate_ref, ya_ref, ob1_ref, ob2_ref, ob3_ref, lb1_ref, lb2_ref, lb3_ref, yc_ref, yd_ref,
                  wa_ref, wb_ref, wc_ref, wd_ref, wo_ref, o_ref):
    l1, l2, l3 = lb1_ref[...], lb2_ref[...], lb3_ref[...]
    m = jnp.maximum(jnp.maximum(l1, l2), l3)
    w1, w2, w3 = jnp.exp(l1 - m), jnp.exp(l2 - m), jnp.exp(l3 - m)
    yb = (w1 * ob1_ref[...] + w2 * ob2_ref[...] + w3 * ob3_ref[...]) / (w1 + w2 + w3)
    branches = (ya_ref[...], yb.astype(BF16), yc_ref[...], yd_ref[...])
    weights = (wa_ref, wb_ref, wc_ref, wd_ref)
    merged = None
    for c in range(N_BRANCHES):
        g = gate_ref[:, c * D_MODEL:(c + 1) * D_MODEL].astype(F32)
        term = g * _dot(branches[c], weights[c][...])
        merged = term if merged is None else merged + term
    o_ref[...] = x_ref[...] + _dot(merged.astype(BF16), wo_ref[...])


def _merge_call(x2d, gate, ya, obs, lbs, yc, yd, wa, wb, wc, wd, wo, tm):
    t_tokens = x2d.shape[0]
    row = lambda i: (i, 0)
    const = lambda i: (0, 0)
    acts = [x2d, gate, ya, *obs, *lbs, yc, yd]
    ws = [wa, wb, wc, wd, wo]
    in_specs = ([pl.BlockSpec((tm, a.shape[1]), row) for a in acts]
                + [pl.BlockSpec(w.shape, const) for w in ws])
    return pl.pallas_call(
        _merge_kernel, grid=(t_tokens // tm,), in_specs=in_specs,
        out_specs=pl.BlockSpec((tm, D_MODEL), row),
        out_shape=jax.ShapeDtypeStruct((t_tokens, D_MODEL), F32),
        compiler_params=pltpu.CompilerParams(dimension_semantics=("arbitrary",), vmem_limit_bytes=VMEM_LIMIT),
        name="merge",
    )(*acts, *ws)


def _mlp_kernel(x_ref, g_ref, wu_ref, wd_ref, o_ref):
    x = x_ref[...]
    hm = _rms_rows(x, g_ref[...]).astype(BF16)
    acc = x
    for c in range(D_MLP // D_MODEL):
        u = _dot(hm, wu_ref[:, c * D_MODEL:(c + 1) * D_MODEL])
        u = jnp.square(jnp.maximum(u, 0.0)).astype(BF16)
        acc = acc + _dot(u, wd_ref[c * D_MODEL:(c + 1) * D_MODEL, :])
    o_ref[...] = acc


def _mlp_call(x2d, g, wu, wd, tm):
    t_tokens = x2d.shape[0]
    row = lambda i: (i, 0)
    const = lambda i: (0, 0)
    single = pl.Buffered(1)
    return pl.pallas_call(
        _mlp_kernel, grid=(t_tokens // tm,),
        in_specs=[pl.BlockSpec((tm, D_MODEL), row), pl.BlockSpec((1, D_MODEL), const),
                  pl.BlockSpec((D_MODEL, D_MLP), const, pipeline_mode=single),
                  pl.BlockSpec((D_MLP, D_MODEL), const, pipeline_mode=single)],
        out_specs=pl.BlockSpec((tm, D_MODEL), row),
        out_shape=jax.ShapeDtypeStruct((t_tokens, D_MODEL), F32),
        compiler_params=pltpu.CompilerParams(dimension_semantics=("arbitrary",), vmem_limit_bytes=VMEM_LIMIT),
        name="mlp",
    )(x2d, g, wu, wd)


def _rope_lane_tables(seq, dim):
    inv = ROPE_THETA ** (-jnp.arange(0, dim, 2, dtype=F32) / dim)
    ang = jnp.arange(seq, dtype=F32)[:, None] * inv[None, :]
    cos, sin = jnp.cos(ang), jnp.sin(ang)
    reps = LANES // dim
    return (jnp.tile(jnp.concatenate([cos, cos], axis=-1), (1, reps)),
            jnp.tile(jnp.concatenate([-sin, sin], axis=-1), (1, reps)))


def _blockdiag_ones(seg):
    idx = np.arange(LANES) // seg
    return jnp.asarray(idx[:, None] == idx[None, :], dtype=BF16)


def _gain_row(l, a_g, b_g, c_g, d_g):
    ones = lambda n: jnp.ones((n,), F32)
    s64, s32 = HEAD_DIM ** -0.5, C_QK_DIM ** -0.5
    parts = [
        jnp.tile(a_g[l, 0] * s64, A_Q_HEADS), jnp.tile(a_g[l, 1], A_KV_HEADS), ones(A_KV_COLS),
        jnp.tile(b_g[l, 0] * s64, B_HEADS), jnp.tile(b_g[l, 1], B_HEADS), ones(B_COLS),
        jnp.tile(c_g[l, 0] * s32, 2 * C_HEADS), jnp.tile(c_g[l, 1], 2 * C_HEADS), ones(C_V_COLS),
        jnp.tile(d_g[l, 0] * s64, D_HEADS), jnp.tile(d_g[l, 1], D_HEADS), ones(D_COLS),
    ]
    return jnp.concatenate(parts).astype(F32)[None, :]


def kernel(x, attn_norm_g, w_in, a_qk_norm_g, a_sink, b_qk_norm_g, c_qk_norm_g, c_lambda, c_subln_g,
           d_qk_norm_g, d_rel_bias, w_branch_a, w_branch_b, w_branch_c, w_branch_d, w_out, mlp_norm_g,
           w_up, w_down):
    b, seq, _ = x.shape
    depth = w_in.shape[0]
    tokens = b * seq
    tm = 256
    cos64, sin64 = _rope_lane_tables(seq, HEAD_DIM)
    cos32, sin32 = _rope_lane_tables(seq, C_QK_DIM)
    tabs = dict(cos64=cos64, sin64=sin64, cos32=cos32, sin32=sin32,
                e64=_blockdiag_ones(HEAD_DIM), e32=_blockdiag_ones(C_QK_DIM))
    xf = x.reshape(tokens, D_MODEL)
    for l in range(depth):
        w_l = w_in[l].astype(BF16)
        gains = _gain_row(l, a_qk_norm_g, b_qk_norm_g, c_qk_norm_g, d_qk_norm_g)
        outs = _proj_call(xf, attn_norm_g[l][None, :], w_l[:, :QKV_COLS], w_l[:, QKV_COLS:], gains, tabs, seq, tm)
        aq, ak, av, bq, bk, bv, cq, ck, cv, dq, dk, dv, gate = outs
        tok3 = lambda t: t.reshape(b, seq, t.shape[-1])
        ya = _window_call(a_sink[l].astype(F32), tok3(aq), tok3(ak), tok3(av), tq=128)
        obs, lbs = [], []
        for gi, (window, dil) in enumerate(B_PAIRS):
            o, lse = _dilated_call(tok3(bq), tok3(bk), tok3(bv), gi, window, dil, tq=128)
            obs.append(o.reshape(tokens, LANES))
            lbs.append(lse.reshape(tokens, LANES))
        lambda_init = 0.8 - 0.6 * math.exp(-0.3 * l)
        subg = jnp.tile(c_subln_g[l].astype(F32), LANES // C_V_DIM)[None, :]
        yc = _diff_call(tok3(cq), tok3(ck), tok3(cv), c_lambda[l].astype(F32), subg, tabs["e64"],
                        lambda_init, tq=128, tk=512)
        yd = _nbr_call(tok3(dq), tok3(dk), tok3(dv), _nbr_bias_table(d_rel_bias[l], seq // GRID_W))
        xf = _merge_call(xf, gate, ya.reshape(tokens, A_Q_COLS), obs, lbs, yc.reshape(tokens, C_V_COLS),
                         yd.reshape(tokens, D_COLS), w_branch_a[l].astype(BF16), w_branch_b[l].astype(BF16),
                         w_branch_c[l].astype(BF16), w_branch_d[l].astype(BF16), w_out[l].astype(BF16), tm)
        xf = _mlp_call(xf, mlp_norm_g[l][None, :], w_up[l].astype(BF16), w_down[l].astype(BF16), tm)
    return xf.reshape(b, seq, D_MODEL)
```

```python
import functools
import math

import jax
import jax.numpy as jnp
import numpy as np
from jax import lax
from jax.experimental import pallas as pl
from jax.experimental.pallas import tpu as pltpu

F32 = jnp.float32
BF16 = jnp.bfloat16

D_MODEL = 1024
HEAD_DIM = 64
GRID_W = 64
ROPE_THETA = 10000.0
NORM_EPS = 1e-6
NEG_INF = -1e30

A_Q_HEADS = 8
A_KV_HEADS = 2
A_RADIUS = 128
B_PAIRS = ((128, 1), (512, 4), (2048, 16))
B_HEADS_PER_GROUP = 2
B_HEADS = B_HEADS_PER_GROUP * len(B_PAIRS)
C_HEADS = 4
C_QK_DIM = 32
C_V_DIM = 2 * C_QK_DIM
D_HEADS = 4
NA_ROWS = 8
NA_COLS = 16
D_MLP = 4 * D_MODEL
N_BRANCHES = 4

A_Q_COLS = A_Q_HEADS * HEAD_DIM
A_KV_COLS = A_KV_HEADS * HEAD_DIM
B_COLS = B_HEADS * HEAD_DIM
C_QK_COLS = C_HEADS * 2 * C_QK_DIM
C_V_COLS = C_HEADS * C_V_DIM
D_COLS = D_HEADS * HEAD_DIM
GATE_COLS = N_BRANCHES * D_MODEL
QKV_COLS = (A_Q_COLS + 2 * A_KV_COLS + 3 * B_COLS + 2 * C_QK_COLS + C_V_COLS + 3 * D_COLS)

LANES = 128
MXU_COLS = 256
BF16_ROWS = 16
VMEM_LIMIT = 48 * 1024 * 1024

_SECTIONS = (
    ("aq", A_Q_COLS, HEAD_DIM, True, BF16), ("ak", A_KV_COLS, HEAD_DIM, True, BF16), ("av", A_KV_COLS, None, False, BF16),
    ("bq", B_COLS, HEAD_DIM, True, F32), ("bk", B_COLS, HEAD_DIM, True, F32), ("bv", B_COLS, None, False, F32),
    ("cq", C_QK_COLS, C_QK_DIM, True, BF16), ("ck", C_QK_COLS, C_QK_DIM, True, BF16), ("cv", C_V_COLS, None, False, BF16),
    ("dq", D_COLS, HEAD_DIM, False, BF16), ("dk", D_COLS, HEAD_DIM, False, BF16), ("dv", D_COLS, None, False, BF16),
)
_SECTION_COL = {s[0]: sum(t[1] for t in _SECTIONS[:i]) for i, s in enumerate(_SECTIONS)}
_TRANSPOSED = ("av", "cv")
_TRANSPOSED_ROW = {n: sum(dict((s[0], s[1]) for s in _SECTIONS)[m] for m in _TRANSPOSED[:i])
                   for i, n in enumerate(_TRANSPOSED)}
_TRANSPOSED_ROWS = sum(s[1] for s in _SECTIONS if s[0] in _TRANSPOSED)
A_HEAD_ORDER = np.array([h for c in range(A_Q_HEADS // A_KV_HEADS) for h in (c, c + A_Q_HEADS // A_KV_HEADS)])
LOG2E = math.log2(math.e)


def _nt_dot(a, b):
    return lax.dot_general(a, b, (((1,), (1,)), ((), ())), preferred_element_type=F32)


def _dot(a, b):
    return jnp.dot(a, b, preferred_element_type=F32)


def _rms_rows(x, g):
    ms = jnp.mean(x * x, axis=-1, keepdims=True)
    return x * lax.rsqrt(ms + NORM_EPS) * g


def _rotate_half(y, half):
    width = y.shape[1]
    lane = lax.broadcasted_iota(jnp.int32, y.shape, 1)
    first = (lane & (2 * half - 1)) < half
    return jnp.where(first, pltpu.roll(y, width - half, 1), pltpu.roll(y, half, 1))


def _proj_kernel(x_ref, g_ref, wqkv_ref, wt_ref, wg_ref, gains_ref, cos64_ref, sin64_ref, cos32_ref, sin32_ref,
                 e64_ref, e32_ref, *out_refs):
    x = x_ref[...]
    h = _rms_rows(x, g_ref[...]).astype(BF16)
    col = 0
    for idx, (name, width, hd, rope, dtype) in enumerate(_SECTIONS):
        o_ref = out_refs[idx]
        if name in _TRANSPOSED:
            r0 = _TRANSPOSED_ROW[name]
            o_ref[...] = _nt_dot(wt_ref[r0:r0 + width, :], h).astype(dtype)
        elif hd is None:
            o_ref[...] = _dot(h, wqkv_ref[:, col:col + width]).astype(dtype)
        else:
            t = _dot(h, wqkv_ref[:, col:col + width])
            e_ref = e64_ref if hd == HEAD_DIM else e32_ref
            cos_ref, sin_ref = (cos64_ref, sin64_ref) if hd == HEAD_DIM else (cos32_ref, sin32_ref)
            for off in range(0, width, MXU_COLS):
                pw = min(MXU_COLS, width - off)
                tc = t[:, off:off + pw]
                ss = _dot((tc * tc).astype(BF16), e_ref[:pw, :pw])
                gain = gains_ref[:, col + off:col + off + pw]
                y = tc * lax.rsqrt(ss * (1.0 / hd) + NORM_EPS) * gain
                if rope:
                    y = y * cos_ref[:, :pw] + _rotate_half(y, hd // 2) * sin_ref[:, :pw]
                o_ref[:, off:off + pw] = y.astype(dtype)
        col += width
    gate_ref = out_refs[len(_SECTIONS)]
    for c in range(N_BRANCHES):
        z = _dot(h, wg_ref[:, c * D_MODEL:(c + 1) * D_MODEL])
        gate_ref[:, c * D_MODEL:(c + 1) * D_MODEL] = (1.0 / (1.0 + jnp.exp(-z))).astype(BF16)


def _proj_call(x2d, g, wqkv, wt, wg, gains, tabs, seq, tm):
    t_tokens = x2d.shape[0]
    n_seq_tiles = seq // tm
    const = lambda i: (0, 0)
    row = lambda i: (i, 0)
    pos = lambda i: (i % n_seq_tiles, 0)
    single = pl.Buffered(1)
    in_specs = [
        pl.BlockSpec((tm, D_MODEL), row),
        pl.BlockSpec((1, D_MODEL), const),
        pl.BlockSpec((D_MODEL, QKV_COLS), const, pipeline_mode=single),
        pl.BlockSpec((_TRANSPOSED_ROWS, D_MODEL), const, pipeline_mode=single),
        pl.BlockSpec((D_MODEL, GATE_COLS), const, pipeline_mode=single),
        pl.BlockSpec((1, QKV_COLS), const),
        pl.BlockSpec((tm, MXU_COLS), pos), pl.BlockSpec((tm, MXU_COLS), pos),
        pl.BlockSpec((tm, MXU_COLS), pos), pl.BlockSpec((tm, MXU_COLS), pos),
        pl.BlockSpec((MXU_COLS, MXU_COLS), const), pl.BlockSpec((MXU_COLS, MXU_COLS), const),
    ]
    out_shape, out_specs = [], []
    for name, w, _, _, dtype in _SECTIONS:
        if name in _TRANSPOSED:
            out_shape.append(jax.ShapeDtypeStruct((w, t_tokens), dtype))
            out_specs.append(pl.BlockSpec((w, tm), lambda i: (0, i)))
        else:
            out_shape.append(jax.ShapeDtypeStruct((t_tokens, w), dtype))
            out_specs.append(pl.BlockSpec((tm, w), row))
    out_shape.append(jax.ShapeDtypeStruct((t_tokens, GATE_COLS), BF16))
    out_specs.append(pl.BlockSpec((tm, GATE_COLS), row))
    return pl.pallas_call(
        _proj_kernel, grid=(t_tokens // tm,), in_specs=in_specs, out_specs=out_specs, out_shape=out_shape,
        compiler_params=pltpu.CompilerParams(dimension_semantics=("arbitrary",), vmem_limit_bytes=VMEM_LIMIT),
        name="proj",
    )(x2d, g, wqkv, wt, wg, gains, tabs["cos64"], tabs["sin64"], tabs["cos32"], tabs["sin32"],
      tabs["e64"], tabs["e32"])


def _band_bias(n, tq, win, radius, start):
    qpos = n * tq + lax.broadcasted_iota(jnp.int32, (tq, win), 0)
    kpos = start + lax.broadcasted_iota(jnp.int32, (tq, win), 1)
    return jnp.where(jnp.abs(kpos - qpos) <= radius, 0.0, NEG_INF).astype(F32)


def _window_kernel(sink_ref, q_ref, k_ref, vt_ref, o_ref, *, tq, seq, radius):
    n = pl.program_id(1)
    win = tq + 2 * radius
    start = pl.multiple_of(jnp.clip(n * tq - radius, 0, seq - win), radius)
    k = k_ref[0, pl.ds(start, win), :]
    vext = jnp.concatenate([vt_ref[:, pl.ds(start, win)], jnp.ones((BF16_ROWS, win), BF16)], axis=0)
    kpos = start + lax.broadcasted_iota(jnp.int32, (win, tq), 0)
    qpos = n * tq + lax.broadcasted_iota(jnp.int32, (win, tq), 1)
    bias = jnp.where(jnp.abs(kpos - qpos) <= radius, 0.0, NEG_INF).astype(F32)
    bias2 = jnp.concatenate([bias, bias], axis=1)
    left = lax.broadcasted_iota(jnp.int32, (tq, LANES), 1) < HEAD_DIM
    first = lax.broadcasted_iota(jnp.int32, (1, 2 * tq), 1) < tq
    rep = A_Q_HEADS // A_KV_HEADS

    def scores(c):
        qc = q_ref[0, :, c * LANES:(c + 1) * LANES]
        zero = jnp.zeros_like(qc)
        qm = jnp.concatenate([jnp.where(left, qc, zero), jnp.where(left, zero, qc)], axis=0)
        return _nt_dot(k, qm) + bias2

    st_next = scores(0)
    for c in range(rep):
        st = st_next
        if c + 1 < rep:
            st_next = scores(c + 1)
        sk = jnp.where(first, sink_ref[c], sink_ref[c + rep]) * LOG2E
        m = jnp.maximum(jnp.max(st, axis=0, keepdims=True), sk)
        pt = jnp.exp2(st - m).astype(BF16)
        acc = _dot(vext, pt)
        den = acc[LANES:LANES + 1] + jnp.exp2(sk - m)
        ot = jnp.concatenate([acc[:HEAD_DIM, :tq] / den[:, :tq], acc[HEAD_DIM:LANES, tq:] / den[:, tq:]], axis=0)
        o_ref[0, :, c * LANES:(c + 1) * LANES] = ot.T.astype(BF16)


def _window_call(sink, q, k, vt, tq):
    b, seq, _ = q.shape
    kern = functools.partial(_window_kernel, tq=tq, seq=seq, radius=A_RADIUS)
    return pl.pallas_call(
        kern, grid=(b, seq // tq),
        in_specs=[pl.BlockSpec(memory_space=pltpu.SMEM),
                  pl.BlockSpec((1, tq, A_Q_COLS), lambda i, n: (i, n, 0)),
                  pl.BlockSpec((1, seq, A_KV_COLS), lambda i, n: (i, 0, 0)),
                  pl.BlockSpec((A_KV_COLS, seq), lambda i, n: (0, i))],
        out_specs=pl.BlockSpec((1, tq, A_Q_COLS), lambda i, n: (i, n, 0)),
        out_shape=jax.ShapeDtypeStruct((b, seq, A_Q_COLS), BF16),
        compiler_params=pltpu.CompilerParams(dimension_semantics=("arbitrary", "arbitrary")),
        name="mixer_a",
    )(sink, q, k, vt)


def _dilated_kernel(q_ref, k_ref, v_ref, o_ref, lse_ref, *, dil, tq, seq, radius):
    sub_len = seq // dil
    n_tiles = sub_len // tq
    win = tq + 2 * radius

    def rows(first, count):
        if dil == 1:
            return pl.ds(pl.multiple_of(first, radius), count)
        return pl.ds(first, count, stride=dil)

    def body(it, carry):
        r = it // n_tiles
        n = it - r * n_tiles
        start = jnp.clip(n * tq - radius, 0, sub_len - win)
        q_rows = rows(r + dil * (n * tq), tq)
        kv_rows = rows(r + dil * start, win)
        q = q_ref[0, q_rows, :].astype(BF16)
        k = k_ref[0, kv_rows, :].astype(BF16)
        v = v_ref[0, kv_rows, :].astype(BF16)
        bias = _band_bias(n, tq, win, radius, start)
        outs, lses = [], []
        head = lambda hh: slice(hh * HEAD_DIM, (hh + 1) * HEAD_DIM)
        scores = [_nt_dot(q[:, head(hh)], k[:, head(hh)]) + bias for hh in range(B_HEADS_PER_GROUP)]
        for hh in range(B_HEADS_PER_GROUP):
            sl = head(hh)
            s = scores[hh]
            m = jnp.max(s, axis=-1, keepdims=True)
            p = jnp.exp(s - m)
            den = jnp.sum(p, axis=-1, keepdims=True)
            outs.append(_dot(p.astype(BF16), v[:, sl]) / den)
            lses.append(jnp.broadcast_to(jnp.log(den) + m, (tq, HEAD_DIM)))
        o_ref[0, q_rows, :] = jnp.concatenate(outs, axis=-1)
        lse_ref[0, q_rows, :] = jnp.concatenate(lses, axis=-1)
        return carry

    lax.fori_loop(0, dil * n_tiles, body, 0)


def _dilated_call(q, k, v, group, window, dil, tq):
    b, seq, _ = q.shape
    radius = window // (2 * dil)
    kern = functools.partial(_dilated_kernel, dil=dil, tq=tq, seq=seq, radius=radius)
    out_sds = jax.ShapeDtypeStruct((b, seq, LANES), F32)
    chunk = lambda i: (i, 0, group)
    whole = lambda i: (i, 0, 0)
    return pl.pallas_call(
        kern, grid=(b,),
        in_specs=[pl.BlockSpec((1, seq, LANES), chunk)] * 3,
        out_specs=[pl.BlockSpec((1, seq, LANES), whole)] * 2,
        out_shape=[out_sds, out_sds],
        compiler_params=pltpu.CompilerParams(dimension_semantics=("arbitrary",)),
        name=f"mixer_b_d{dil}",
    )(q, k, v)


def _diff_kernel(q_ref, k_ref, vt_ref, lam_ref, subg_ref, o_ref, *, tq, tk, seq, lambda_init):
    nseg = LANES // C_QK_DIM
    q = q_ref[0]
    lane = lax.broadcasted_iota(jnp.int32, (tq, LANES), 1)
    zero = jnp.zeros_like(q)
    q4 = jnp.concatenate([jnp.where((lane // C_QK_DIM) == j, q, zero) for j in range(nseg)], axis=0)
    ones_rows = jnp.ones((BF16_ROWS, tk), BF16)
    m = jnp.full((1, nseg * tq), NEG_INF, F32)
    acc = jnp.zeros((LANES + BF16_ROWS, nseg * tq), F32)
    n_blocks = seq // tk
    scores = lambda t: _nt_dot(k_ref[0, t * tk:(t + 1) * tk, :], q4)
    ahead = 2
    pending = [scores(t) for t in range(min(ahead, n_blocks))]
    for t in range(n_blocks):
        st = pending.pop(0)
        if t + ahead < n_blocks:
            pending.append(scores(t + ahead))
        m_new = jnp.maximum(m, jnp.max(st, axis=0, keepdims=True))
        pt = jnp.exp2(st - m_new).astype(BF16)
        alpha = jnp.exp2(m - m_new)
        vext = jnp.concatenate([vt_ref[:, t * tk:(t + 1) * tk], ones_rows], axis=0)
        acc = acc * alpha + _dot(vext, pt)
        m = m_new

    lp = lam_ref[...]
    lam = (jnp.exp(jnp.sum(lp[0:1] * lp[1:2], axis=-1, keepdims=True))
           - jnp.exp(jnp.sum(lp[2:3] * lp[3:4], axis=-1, keepdims=True)) + lambda_init)
    heads = []
    for hh in range(LANES // C_V_DIM):
        u = []
        for c in range(2):
            j = 2 * hh + c
            cols = slice(j * tq, (j + 1) * tq)
            u.append(acc[hh * C_V_DIM:(hh + 1) * C_V_DIM, cols] / acc[LANES:LANES + 1, cols])
        o = u[0] - lam * u[1]
        ss = jnp.sum(o * o, axis=0, keepdims=True)
        heads.append(o * lax.rsqrt(ss * (1.0 / C_V_DIM) + NORM_EPS))
    ot = jnp.concatenate(heads, axis=0) * subg_ref[...] * (1.0 - lambda_init)
    o_ref[0] = ot.T.astype(BF16)


def _diff_call(q, k, vt, lam_p, subg, lambda_init, tq, tk):
    b, seq, _ = q.shape
    pairs = C_QK_COLS // LANES
    kern = functools.partial(_diff_kernel, tq=tq, tk=tk, seq=seq, lambda_init=lambda_init)
    return pl.pallas_call(
        kern, grid=(b, pairs, seq // tq),
        in_specs=[pl.BlockSpec((1, tq, LANES), lambda i, p, n: (i, n, p)),
                  pl.BlockSpec((1, seq, LANES), lambda i, p, n: (i, 0, p)),
                  pl.BlockSpec((LANES, seq), lambda i, p, n: (p, i)),
                  pl.BlockSpec((4, C_QK_DIM), lambda i, p, n: (0, 0)),
                  pl.BlockSpec((LANES, tq), lambda i, p, n: (0, 0))],
        out_specs=pl.BlockSpec((1, tq, LANES), lambda i, p, n: (i, n, p)),
        out_shape=jax.ShapeDtypeStruct((b, seq, C_V_COLS), BF16),
        compiler_params=pltpu.CompilerParams(dimension_semantics=("arbitrary",) * 3),
        name="mixer_c",
    )(q, k, vt, lam_p, subg)


def _nbr_kernel(q_ref, k_ref, v_ref, bias_ref, o_ref, *, rows):
    kr = NA_ROWS
    lane = lax.broadcasted_iota(jnp.int32, (GRID_W, LANES), 1)
    left = lane < HEAD_DIM

    def body(i, carry):
        r0 = jnp.clip(i - kr // 2, 0, rows - kr)
        cls = jnp.minimum(i, kr // 2) + jnp.maximum(i - (rows - kr // 2), 0)
        q = q_ref[0, pl.ds(pl.multiple_of(i * GRID_W, GRID_W), GRID_W), :]
        koff = pl.multiple_of(r0 * GRID_W, GRID_W)
        k = k_ref[0, pl.ds(koff, kr * GRID_W), :]
        v = v_ref[0, pl.ds(koff, kr * GRID_W), :]
        zero = jnp.zeros_like(q)
        q2 = jnp.concatenate([jnp.where(left, q, zero), jnp.where(left, zero, q)], axis=0)
        bias = jnp.concatenate(
            [jnp.concatenate([bias_ref[hh, NA_ROWS - 1 - cls + 2 * u] for u in range(kr // 2)], axis=1)
             for hh in range(2)], axis=0)
        s = _nt_dot(q2, k) + bias
        m = jnp.max(s, axis=-1, keepdims=True)
        p = jnp.exp(s - m)
        den = jnp.sum(p, axis=-1, keepdims=True)
        pv = _dot(p.astype(BF16), v) / den
        o = jnp.where(left, pv[:GRID_W], pv[GRID_W:])
        o_ref[0, pl.ds(pl.multiple_of(i * GRID_W, GRID_W), GRID_W), :] = o.astype(BF16)
        return carry

    lax.fori_loop(0, rows, body, 0)


def _nbr_call(q, k, v, bias_tab):
    b, seq, _ = q.shape
    rows = seq // GRID_W
    pairs = D_COLS // LANES
    kern = functools.partial(_nbr_kernel, rows=rows)
    blk = lambda i, p: (i, 0, p)
    return pl.pallas_call(
        kern, grid=(b, pairs),
        in_specs=[pl.BlockSpec((1, seq, LANES), blk), pl.BlockSpec((1, seq, LANES), blk),
                  pl.BlockSpec((1, seq, LANES), blk),
                  pl.BlockSpec((2,) + bias_tab.shape[1:], lambda i, p: (p, 0, 0, 0))],
        out_specs=pl.BlockSpec((1, seq, LANES), blk),
        out_shape=jax.ShapeDtypeStruct((b, seq, D_COLS), BF16),
        compiler_params=pltpu.CompilerParams(dimension_semantics=("arbitrary",) * 2),
        name="mixer_d",
    )(q, k, v, bias_tab)


def _nbr_bias_table(rpb):
    cj = np.arange(GRID_W)
    c0 = np.clip(cj - NA_COLS // 2, 0, GRID_W - NA_COLS)
    col_ok = (cj[None, :] >= c0[:, None]) & (cj[None, :] < c0[:, None] + NA_COLS)
    dc = np.clip(cj[None, :] - cj[:, None], -(NA_COLS - 1), NA_COLS - 1) + (NA_COLS - 1)
    per_row = jnp.where(col_ok[None, None], rpb.astype(F32)[:, :, dc], NEG_INF)
    return jnp.concatenate([per_row[:, :-1], per_row[:, 1:]], axis=-1)


def _merge_kernel(x_ref, gate_ref, ya_ref, ob1_ref, ob2_ref, ob3_ref, lb1_ref, lb2_ref, lb3_ref, yc_ref, yd_ref,
                  wa_ref, wb_ref, wc_ref, wd_ref, wo_ref, o_ref):
    l1, l2, l3 = lb1_ref[...], lb2_ref[...], lb3_ref[...]
    m = jnp.maximum(jnp.maximum(l1, l2), l3)
    w1, w2, w3 = jnp.exp(l1 - m), jnp.exp(l2 - m), jnp.exp(l3 - m)
    yb = (w1 * ob1_ref[...] + w2 * ob2_ref[...] + w3 * ob3_ref[...]) / (w1 + w2 + w3)
    branches = (ya_ref[...], yb.astype(BF16), yc_ref[...], yd_ref[...])
    weights = (wa_ref, wb_ref, wc_ref, wd_ref)
    merged = None
    for c in range(N_BRANCHES):
        g = gate_ref[:, c * D_MODEL:(c + 1) * D_MODEL].astype(F32)
        term = g * _dot(branches[c], weights[c][...])
        merged = term if merged is None else merged + term
    o_ref[...] = x_ref[...] + _dot(merged.astype(BF16), wo_ref[...])


def _merge_call(x2d, gate, ya, obs, lbs, yc, yd, wa, wb, wc, wd, wo, tm):
    t_tokens = x2d.shape[0]
    row = lambda i: (i, 0)
    const = lambda i: (0, 0)
    acts = [x2d, gate, ya, *obs, *lbs, yc, yd]
    ws = [wa, wb, wc, wd, wo]
    in_specs = ([pl.BlockSpec((tm, a.shape[1]), row) for a in acts]
                + [pl.BlockSpec(w.shape, const) for w in ws])
    return pl.pallas_call(
        _merge_kernel, grid=(t_tokens // tm,), in_specs=in_specs,
        out_specs=pl.BlockSpec((tm, D_MODEL), row),
        out_shape=jax.ShapeDtypeStruct((t_tokens, D_MODEL), F32),
        compiler_params=pltpu.CompilerParams(dimension_semantics=("arbitrary",), vmem_limit_bytes=VMEM_LIMIT),
        name="merge",
    )(*acts, *ws)


def _mlp_kernel(x_ref, g_ref, wu_ref, wd_ref, o_ref):
    x = x_ref[...]
    hm = _rms_rows(x, g_ref[...]).astype(BF16)
    acc = x
    for c in range(D_MLP // D_MODEL):
        u = _dot(hm, wu_ref[:, c * D_MODEL:(c + 1) * D_MODEL])
        u = jnp.square(jnp.maximum(u, 0.0)).astype(BF16)
        acc = acc + _dot(u, wd_ref[c * D_MODEL:(c + 1) * D_MODEL, :])
    o_ref[...] = acc


def _mlp_call(x2d, g, wu, wd, tm):
    t_tokens = x2d.shape[0]
    row = lambda i: (i, 0)
    const = lambda i: (0, 0)
    single = pl.Buffered(1)
    return pl.pallas_call(
        _mlp_kernel, grid=(t_tokens // tm,),
        in_specs=[pl.BlockSpec((tm, D_MODEL), row), pl.BlockSpec((1, D_MODEL), const),
                  pl.BlockSpec((D_MODEL, D_MLP), const, pipeline_mode=single),
                  pl.BlockSpec((D_MLP, D_MODEL), const, pipeline_mode=single)],
        out_specs=pl.BlockSpec((tm, D_MODEL), row),
        out_shape=jax.ShapeDtypeStruct((t_tokens, D_MODEL), F32),
        compiler_params=pltpu.CompilerParams(dimension_semantics=("arbitrary",), vmem_limit_bytes=VMEM_LIMIT),
        name="mlp",
    )(x2d, g, wu, wd)


def _rope_lane_tables(seq, dim):
    inv = ROPE_THETA ** (-jnp.arange(0, dim, 2, dtype=F32) / dim)
    ang = jnp.arange(seq, dtype=F32)[:, None] * inv[None, :]
    cos, sin = jnp.cos(ang), jnp.sin(ang)
    reps = MXU_COLS // dim
    return (jnp.tile(jnp.concatenate([cos, cos], axis=-1), (1, reps)),
            jnp.tile(jnp.concatenate([-sin, sin], axis=-1), (1, reps)))


def _blockdiag_ones(seg):
    idx = np.arange(MXU_COLS) // seg
    return jnp.asarray(idx[:, None] == idx[None, :], dtype=BF16)


def _gain_row(l, a_g, b_g, c_g, d_g):
    ones = lambda n: jnp.ones((n,), F32)
    s64, s32 = HEAD_DIM ** -0.5, C_QK_DIM ** -0.5
    parts = [
        jnp.tile(a_g[l, 0] * (s64 * LOG2E), A_Q_HEADS), jnp.tile(a_g[l, 1], A_KV_HEADS), ones(A_KV_COLS),
        jnp.tile(b_g[l, 0] * s64, B_HEADS), jnp.tile(b_g[l, 1], B_HEADS), ones(B_COLS),
        jnp.tile(c_g[l, 0] * (s32 * LOG2E), 2 * C_HEADS), jnp.tile(c_g[l, 1], 2 * C_HEADS), ones(C_V_COLS),
        jnp.tile(d_g[l, 0] * s64, D_HEADS), jnp.tile(d_g[l, 1], D_HEADS), ones(D_COLS),
    ]
    return jnp.concatenate(parts).astype(F32)[None, :]


def kernel(x, attn_norm_g, w_in, a_qk_norm_g, a_sink, b_qk_norm_g, c_qk_norm_g, c_lambda, c_subln_g,
           d_qk_norm_g, d_rel_bias, w_branch_a, w_branch_b, w_branch_c, w_branch_d, w_out, mlp_norm_g,
           w_up, w_down):
    b, seq, _ = x.shape
    depth = w_in.shape[0]
    tokens = b * seq
    tm = 256
    tq_c = 256
    cos64, sin64 = _rope_lane_tables(seq, HEAD_DIM)
    cos32, sin32 = _rope_lane_tables(seq, C_QK_DIM)
    tabs = dict(cos64=cos64, sin64=sin64, cos32=cos32, sin32=sin32,
                e64=_blockdiag_ones(HEAD_DIM), e32=_blockdiag_ones(C_QK_DIM))
    xf = x.reshape(tokens, D_MODEL)
    for l in range(depth):
        w_l = w_in[l].astype(BF16)
        gains = _gain_row(l, a_qk_norm_g, b_qk_norm_g, c_qk_norm_g, d_qk_norm_g)
        w_aq = w_l[:, :A_Q_COLS].reshape(D_MODEL, A_Q_HEADS, HEAD_DIM)[:, A_HEAD_ORDER, :].reshape(D_MODEL, A_Q_COLS)
        wqkv = jnp.concatenate([w_aq, w_l[:, A_Q_COLS:QKV_COLS]], axis=1)
        wt = jnp.concatenate([w_l[:, _SECTION_COL[n]:_SECTION_COL[n] + dict((s[0], s[1]) for s in _SECTIONS)[n]].T
                              for n in _TRANSPOSED], axis=0)
        w_ba = (w_branch_a[l].astype(BF16).reshape(A_Q_HEADS, HEAD_DIM, D_MODEL)[A_HEAD_ORDER, :, :]
                .reshape(A_Q_COLS, D_MODEL))
        outs = _proj_call(xf, attn_norm_g[l][None, :], wqkv, wt, w_l[:, QKV_COLS:], gains, tabs, seq, tm)
        aq, ak, avt, bq, bk, bv, cq, ck, cvt, dq, dk, dv, gate = outs
        tok3 = lambda t: t.reshape(b, seq, t.shape[-1])
        ya = _window_call(a_sink[l].astype(F32), tok3(aq), tok3(ak), avt, tq=128)
        obs, lbs = [], []
        for gi, (window, dil) in enumerate(B_PAIRS):
            o, lse = _dilated_call(tok3(bq), tok3(bk), tok3(bv), gi, window, dil, tq=128)
            obs.append(o.reshape(tokens, LANES))
            lbs.append(lse.reshape(tokens, LANES))
        lambda_init = 0.8 - 0.6 * math.exp(-0.3 * l)
        subg = jnp.broadcast_to(jnp.tile(c_subln_g[l].astype(F32), LANES // C_V_DIM)[:, None], (LANES, tq_c))
        yc = _diff_call(tok3(cq), tok3(ck), cvt, c_lambda[l].astype(F32), subg, lambda_init, tq=tq_c, tk=512)
        yd = _nbr_call(tok3(dq), tok3(dk), tok3(dv), _nbr_bias_table(d_rel_bias[l]))
        xf = _merge_call(xf, gate, ya.reshape(tokens, A_Q_COLS), obs, lbs, yc.reshape(tokens, C_V_COLS),
                         yd.reshape(tokens, D_COLS), w_ba, w_branch_b[l].astype(BF16),
                         w_branch_c[l].astype(BF16), w_branch_d[l].astype(BF16), w_out[l].astype(BF16), tm)
        xf = _mlp_call(xf, mlp_norm_g[l][None, :], w_up[l].astype(BF16), w_down[l].astype(BF16), tm)
    return xf.reshape(b, seq, D_MODEL)
```

```python
import functools
import math

import jax
import jax.numpy as jnp
import numpy as np
from jax import lax
from jax.experimental import pallas as pl
from jax.experimental.pallas import tpu as pltpu

F32 = jnp.float32
BF16 = jnp.bfloat16

D_MODEL = 1024
HEAD_DIM = 64
GRID_W = 64
ROPE_THETA = 10000.0
NORM_EPS = 1e-6
NEG_INF = -1e30

A_Q_HEADS = 8
A_KV_HEADS = 2
A_RADIUS = 128
B_PAIRS = ((128, 1), (512, 4), (2048, 16))
B_HEADS_PER_GROUP = 2
B_HEADS = B_HEADS_PER_GROUP * len(B_PAIRS)
C_HEADS = 4
C_QK_DIM = 32
C_V_DIM = 2 * C_QK_DIM
D_HEADS = 4
NA_ROWS = 8
NA_COLS = 16
D_MLP = 4 * D_MODEL
N_BRANCHES = 4

A_Q_COLS = A_Q_HEADS * HEAD_DIM
A_KV_COLS = A_KV_HEADS * HEAD_DIM
B_COLS = B_HEADS * HEAD_DIM
C_QK_COLS = C_HEADS * 2 * C_QK_DIM
C_V_COLS = C_HEADS * C_V_DIM
D_COLS = D_HEADS * HEAD_DIM
GATE_COLS = N_BRANCHES * D_MODEL
QKV_COLS = (A_Q_COLS + 2 * A_KV_COLS + 3 * B_COLS + 2 * C_QK_COLS + C_V_COLS + 3 * D_COLS)

LANES = 128
MXU_COLS = 256
BF16_ROWS = 16
VMEM_LIMIT = 48 * 1024 * 1024

_SECTIONS = (
    ("aq", A_Q_COLS, HEAD_DIM, True, BF16), ("ak", A_KV_COLS, HEAD_DIM, True, BF16), ("av", A_KV_COLS, None, False, BF16),
    ("bq", B_COLS, HEAD_DIM, True, F32), ("bk", B_COLS, HEAD_DIM, True, F32), ("bv", B_COLS, None, False, F32),
    ("cq", C_QK_COLS, C_QK_DIM, True, BF16), ("ck", C_QK_COLS, C_QK_DIM, True, BF16), ("cv", C_V_COLS, None, False, BF16),
    ("dq", D_COLS, HEAD_DIM, False, BF16), ("dk", D_COLS, HEAD_DIM, False, BF16), ("dv", D_COLS, None, False, BF16),
)
_SECTION_COL = {s[0]: sum(t[1] for t in _SECTIONS[:i]) for i, s in enumerate(_SECTIONS)}
_TRANSPOSED = ("av", "cv")
_TRANSPOSED_ROW = {n: sum(dict((s[0], s[1]) for s in _SECTIONS)[m] for m in _TRANSPOSED[:i])
                   for i, n in enumerate(_TRANSPOSED)}
_TRANSPOSED_ROWS = sum(s[1] for s in _SECTIONS if s[0] in _TRANSPOSED)
A_HEAD_ORDER = np.array([h for c in range(A_Q_HEADS // A_KV_HEADS) for h in (c, c + A_Q_HEADS // A_KV_HEADS)])
LOG2E = math.log2(math.e)


def _nt_dot(a, b):
    return lax.dot_general(a, b, (((1,), (1,)), ((), ())), preferred_element_type=F32)


def _dot(a, b):
    return jnp.dot(a, b, preferred_element_type=F32)


def _rms_rows(x, g):
    ms = jnp.mean(x * x, axis=-1, keepdims=True)
    return x * lax.rsqrt(ms + NORM_EPS) * g


def _rotate_half(y, half):
    width = y.shape[1]
    lane = lax.broadcasted_iota(jnp.int32, y.shape, 1)
    first = (lane & (2 * half - 1)) < half
    return jnp.where(first, pltpu.roll(y, width - half, 1), pltpu.roll(y, half, 1))


def _proj_kernel(x_ref, g_ref, wqkv_ref, wt_ref, gains_ref, cos64_ref, sin64_ref, cos32_ref, sin32_ref,
                 e64_ref, e32_ref, *out_refs):
    x = x_ref[...]
    h = _rms_rows(x, g_ref[...]).astype(BF16)

    def project(idx):
        name, width = _SECTIONS[idx][:2]
        if name in _TRANSPOSED:
            r0 = _TRANSPOSED_ROW[name]
            return _nt_dot(wt_ref[r0:r0 + width, :], h)
        return _dot(h, wqkv_ref[:, _SECTION_COL[name]:_SECTION_COL[name] + width])

    t_next = project(0)
    for idx, (name, width, hd, rope, dtype) in enumerate(_SECTIONS):
        o_ref = out_refs[idx]
        col = _SECTION_COL[name]
        t = t_next
        if idx + 1 < len(_SECTIONS):
            t_next = project(idx + 1)
        if hd is None:
            o_ref[...] = t.astype(dtype)
        else:
            e_ref = e64_ref if hd == HEAD_DIM else e32_ref
            cos_ref, sin_ref = (cos64_ref, sin64_ref) if hd == HEAD_DIM else (cos32_ref, sin32_ref)
            for off in range(0, width, MXU_COLS):
                pw = min(MXU_COLS, width - off)
                tc = t[:, off:off + pw]
                ss = _dot((tc * tc).astype(BF16), e_ref[:pw, :pw])
                gain = gains_ref[:, col + off:col + off + pw]
                y = tc * lax.rsqrt(ss * (1.0 / hd) + NORM_EPS) * gain
                if rope:
                    y = y * cos_ref[:, :pw] + _rotate_half(y, hd // 2) * sin_ref[:, :pw]
                o_ref[:, off:off + pw] = y.astype(dtype)


def _proj_call(x2d, g, wqkv, wt, gains, tabs, seq, tm):
    t_tokens = x2d.shape[0]
    n_seq_tiles = seq // tm
    const = lambda i: (0, 0)
    row = lambda i: (i, 0)
    pos = lambda i: (i % n_seq_tiles, 0)
    single = pl.Buffered(1)
    in_specs = [
        pl.BlockSpec((tm, D_MODEL), row),
        pl.BlockSpec((1, D_MODEL), const),
        pl.BlockSpec((D_MODEL, QKV_COLS), const, pipeline_mode=single),
        pl.BlockSpec((_TRANSPOSED_ROWS, D_MODEL), const, pipeline_mode=single),
        pl.BlockSpec((1, QKV_COLS), const),
        pl.BlockSpec((tm, MXU_COLS), pos), pl.BlockSpec((tm, MXU_COLS), pos),
        pl.BlockSpec((tm, MXU_COLS), pos), pl.BlockSpec((tm, MXU_COLS), pos),
        pl.BlockSpec((MXU_COLS, MXU_COLS), const), pl.BlockSpec((MXU_COLS, MXU_COLS), const),
    ]
    out_shape, out_specs = [], []
    for name, w, _, _, dtype in _SECTIONS:
        if name in _TRANSPOSED:
            out_shape.append(jax.ShapeDtypeStruct((w, t_tokens), dtype))
            out_specs.append(pl.BlockSpec((w, tm), lambda i: (0, i)))
        else:
            out_shape.append(jax.ShapeDtypeStruct((t_tokens, w), dtype))
            out_specs.append(pl.BlockSpec((tm, w), row))
    return pl.pallas_call(
        _proj_kernel, grid=(t_tokens // tm,), in_specs=in_specs, out_specs=out_specs, out_shape=out_shape,
        compiler_params=pltpu.CompilerParams(dimension_semantics=("arbitrary",), vmem_limit_bytes=VMEM_LIMIT),
        name="proj",
    )(x2d, g, wqkv, wt, gains, tabs["cos64"], tabs["sin64"], tabs["cos32"], tabs["sin32"],
      tabs["e64"], tabs["e32"])


def _band_bias(n, tq, win, radius, start):
    qpos = n * tq + lax.broadcasted_iota(jnp.int32, (tq, win), 0)
    kpos = start + lax.broadcasted_iota(jnp.int32, (tq, win), 1)
    return jnp.where(jnp.abs(kpos - qpos) <= radius, 0.0, NEG_INF).astype(F32)


def _window_kernel(sink_ref, q_ref, k_ref, vt_ref, o_ref, *, tq, seq, radius):
    n = pl.program_id(1)
    win = tq + 2 * radius
    start = pl.multiple_of(jnp.clip(n * tq - radius, 0, seq - win), radius)
    k = k_ref[0, pl.ds(start, win), :]
    vext = jnp.concatenate([vt_ref[:, pl.ds(start, win)], jnp.ones((BF16_ROWS, win), BF16)], axis=0)
    kpos = start + lax.broadcasted_iota(jnp.int32, (win, tq), 0)
    qpos = n * tq + lax.broadcasted_iota(jnp.int32, (win, tq), 1)
    bias = jnp.where(jnp.abs(kpos - qpos) <= radius, 0.0, NEG_INF).astype(F32)
    bias2 = jnp.concatenate([bias, bias], axis=1)
    left = lax.broadcasted_iota(jnp.int32, (tq, LANES), 1) < HEAD_DIM
    first = lax.broadcasted_iota(jnp.int32, (1, 2 * tq), 1) < tq
    rep = A_Q_HEADS // A_KV_HEADS

    def scores(c):
        qc = q_ref[0, :, c * LANES:(c + 1) * LANES]
        zero = jnp.zeros_like(qc)
        qm = jnp.concatenate([jnp.where(left, qc, zero), jnp.where(left, zero, qc)], axis=0)
        return _nt_dot(k, qm) + bias2

    ahead = 1
    pending = [scores(c) for c in range(ahead)]
    for c in range(rep):
        st = pending.pop(0)
        if c + ahead < rep:
            pending.append(scores(c + ahead))
        sk = jnp.where(first, sink_ref[c], sink_ref[c + rep]) * LOG2E
        m = jnp.maximum(jnp.max(st, axis=0, keepdims=True), sk)
        pt = jnp.exp2(st - m).astype(BF16)
        acc = _dot(vext, pt)
        den = acc[LANES:LANES + 1] + jnp.exp2(sk - m)
        ot = jnp.concatenate([acc[:HEAD_DIM, :tq] / den[:, :tq], acc[HEAD_DIM:LANES, tq:] / den[:, tq:]], axis=0)
        o_ref[0, :, c * LANES:(c + 1) * LANES] = ot.T.astype(BF16)


def _window_call(sink, q, k, vt, tq):
    b, seq, _ = q.shape
    kern = functools.partial(_window_kernel, tq=tq, seq=seq, radius=A_RADIUS)
    return pl.pallas_call(
        kern, grid=(b, seq // tq),
        in_specs=[pl.BlockSpec(memory_space=pltpu.SMEM),
                  pl.BlockSpec((1, tq, A_Q_COLS), lambda i, n: (i, n, 0)),
                  pl.BlockSpec((1, seq, A_KV_COLS), lambda i, n: (i, 0, 0)),
                  pl.BlockSpec((A_KV_COLS, seq), lambda i, n: (0, i))],
        out_specs=pl.BlockSpec((1, tq, A_Q_COLS), lambda i, n: (i, n, 0)),
        out_shape=jax.ShapeDtypeStruct((b, seq, A_Q_COLS), BF16),
        compiler_params=pltpu.CompilerParams(dimension_semantics=("arbitrary", "arbitrary")),
        name="mixer_a",
    )(sink, q, k, vt)


def _dilated_kernel(q_ref, k_ref, v_ref, o_ref, lse_ref, *, dil, tq, seq, radius):
    sub_len = seq // dil
    n_tiles = sub_len // tq
    win = tq + 2 * radius

    def rows(first, count):
        if dil == 1:
            return pl.ds(pl.multiple_of(first, radius), count)
        return pl.ds(first, count, stride=dil)

    left = lax.broadcasted_iota(jnp.int32, (tq, LANES), 1) < HEAD_DIM
    ones = jnp.ones((win, LANES), BF16)

    def scores(it):
        r = it // n_tiles
        n = it - r * n_tiles
        start = jnp.clip(n * tq - radius, 0, sub_len - win)
        q_rows = rows(r + dil * (n * tq), tq)
        kv_rows = rows(r + dil * start, win)
        q = q_ref[0, q_rows, :].astype(BF16)
        k = k_ref[0, kv_rows, :].astype(BF16)
        v = v_ref[0, kv_rows, :].astype(BF16)
        zero = jnp.zeros_like(q)
        q2 = jnp.concatenate([jnp.where(left, q, zero), jnp.where(left, zero, q)], axis=0)
        bias = _band_bias(n, tq, win, radius, start)
        return q_rows, v, _nt_dot(q2, k) + jnp.concatenate([bias, bias], axis=0)

    def finish(q_rows, v, s):
        m = jnp.max(s, axis=-1, keepdims=True)
        p = jnp.exp(s - m).astype(BF16)
        acc = _dot(p, jnp.concatenate([v, ones], axis=1))
        den = acc[:, LANES:]
        o2 = acc[:, :LANES] / den
        lse2 = jnp.log(den) + m
        o_ref[0, q_rows, :] = jnp.where(left, o2[:tq], o2[tq:])
        lse_ref[0, q_rows, :] = jnp.where(left, lse2[:tq], lse2[tq:])

    blocks_per_iter = 4

    def body(j, carry):
        pending = [scores(blocks_per_iter * j + u) for u in range(blocks_per_iter)]
        for blk in pending:
            finish(*blk)
        return carry

    lax.fori_loop(0, dil * n_tiles // blocks_per_iter, body, 0)


def _dilated_call(q, k, v, group, window, dil, tq):
    b, seq, _ = q.shape
    radius = window // (2 * dil)
    kern = functools.partial(_dilated_kernel, dil=dil, tq=tq, seq=seq, radius=radius)
    out_sds = jax.ShapeDtypeStruct((b, seq, LANES), F32)
    chunk = lambda i: (i, 0, group)
    whole = lambda i: (i, 0, 0)
    return pl.pallas_call(
        kern, grid=(b,),
        in_specs=[pl.BlockSpec((1, seq, LANES), chunk)] * 3,
        out_specs=[pl.BlockSpec((1, seq, LANES), whole)] * 2,
        out_shape=[out_sds, out_sds],
        compiler_params=pltpu.CompilerParams(dimension_semantics=("arbitrary",)),
        name=f"mixer_b_d{dil}",
    )(q, k, v)


def _diff_kernel(q_ref, k_ref, vt_ref, lam_ref, subg_ref, o_ref, *, tq, tk, seq, lambda_init):
    nseg = LANES // C_QK_DIM
    q = q_ref[0]
    lane = lax.broadcasted_iota(jnp.int32, (tq, LANES), 1)
    zero = jnp.zeros_like(q)
    q4 = jnp.concatenate([jnp.where((lane // C_QK_DIM) == j, q, zero) for j in range(nseg)], axis=0)
    ones_rows = jnp.ones((BF16_ROWS, tk), BF16)
    m = jnp.full((1, nseg * tq), NEG_INF, F32)
    acc = jnp.zeros((LANES + BF16_ROWS, nseg * tq), F32)
    n_blocks = seq // tk
    scores = lambda t: _nt_dot(k_ref[0, t * tk:(t + 1) * tk, :], q4)
    ahead = 2
    pending = [scores(t) for t in range(min(ahead, n_blocks))]
    for t in range(n_blocks):
        st = pending.pop(0)
        if t + ahead < n_blocks:
            pending.append(scores(t + ahead))
        m_new = jnp.maximum(m, jnp.max(st, axis=0, keepdims=True))
        pt = jnp.exp2(st - m_new).astype(BF16)
        alpha = jnp.exp2(m - m_new)
        vext = jnp.concatenate([vt_ref[:, t * tk:(t + 1) * tk], ones_rows], axis=0)
        acc = acc * alpha + _dot(vext, pt)
        m = m_new

    lp = lam_ref[...]
    lam = (jnp.exp(jnp.sum(lp[0:1] * lp[1:2], axis=-1, keepdims=True))
           - jnp.exp(jnp.sum(lp[2:3] * lp[3:4], axis=-1, keepdims=True)) + lambda_init)
    heads = []
    for hh in range(LANES // C_V_DIM):
        u = []
        for c in range(2):
            j = 2 * hh + c
            cols = slice(j * tq, (j + 1) * tq)
            u.append(acc[hh * C_V_DIM:(hh + 1) * C_V_DIM, cols] / acc[LANES:LANES + 1, cols])
        o = u[0] - lam * u[1]
        ss = jnp.sum(o * o, axis=0, keepdims=True)
        heads.append(o * lax.rsqrt(ss * (1.0 / C_V_DIM) + NORM_EPS))
    ot = jnp.concatenate(heads, axis=0) * subg_ref[...] * (1.0 - lambda_init)
    o_ref[0] = ot.T.astype(BF16)


def _diff_call(q, k, vt, lam_p, subg, lambda_init, tq, tk):
    b, seq, _ = q.shape
    pairs = C_QK_COLS // LANES
    kern = functools.partial(_diff_kernel, tq=tq, tk=tk, seq=seq, lambda_init=lambda_init)
    return pl.pallas_call(
        kern, grid=(b, pairs, seq // tq),
        in_specs=[pl.BlockSpec((1, tq, LANES), lambda i, p, n: (i, n, p)),
                  pl.BlockSpec((1, seq, LANES), lambda i, p, n: (i, 0, p)),
                  pl.BlockSpec((LANES, seq), lambda i, p, n: (p, i)),
                  pl.BlockSpec((4, C_QK_DIM), lambda i, p, n: (0, 0)),
                  pl.BlockSpec((LANES, tq), lambda i, p, n: (0, 0))],
        out_specs=pl.BlockSpec((1, tq, LANES), lambda i, p, n: (i, n, p)),
        out_shape=jax.ShapeDtypeStruct((b, seq, C_V_COLS), BF16),
        compiler_params=pltpu.CompilerParams(dimension_semantics=("arbitrary",) * 3),
        name="mixer_c",
    )(q, k, vt, lam_p, subg)


def _nbr_kernel(q_ref, k_ref, v_ref, bias_ref, o_ref, *, rows):
    kr = NA_ROWS
    lane = lax.broadcasted_iota(jnp.int32, (GRID_W, LANES), 1)
    left = lane < HEAD_DIM

    ones = jnp.ones((kr * GRID_W, LANES), BF16)

    def scores(i):
        r0 = jnp.clip(i - kr // 2, 0, rows - kr)
        cls = jnp.minimum(i, kr // 2) + jnp.maximum(i - (rows - kr // 2), 0)
        q = q_ref[0, pl.ds(pl.multiple_of(i * GRID_W, GRID_W), GRID_W), :]
        koff = pl.multiple_of(r0 * GRID_W, GRID_W)
        k = k_ref[0, pl.ds(koff, kr * GRID_W), :]
        v = v_ref[0, pl.ds(koff, kr * GRID_W), :]
        zero = jnp.zeros_like(q)
        q2 = jnp.concatenate([jnp.where(left, q, zero), jnp.where(left, zero, q)], axis=0)
        bias = jnp.concatenate(
            [jnp.concatenate([bias_ref[hh, NA_ROWS - 1 - cls + 2 * u] for u in range(kr // 2)], axis=1)
             for hh in range(2)], axis=0)
        return i, v, _nt_dot(q2, k) + bias

    def finish(i, v, s):
        m = jnp.max(s, axis=-1, keepdims=True)
        p = jnp.exp(s - m).astype(BF16)
        acc = _dot(p, jnp.concatenate([v, ones], axis=1))
        pv = acc[:, :LANES] / acc[:, LANES:]
        o = jnp.where(left, pv[:GRID_W], pv[GRID_W:])
        o_ref[0, pl.ds(pl.multiple_of(i * GRID_W, GRID_W), GRID_W), :] = o.astype(BF16)

    rows_per_iter = 4

    def body(j, carry):
        pending = [scores(rows_per_iter * j + u) for u in range(rows_per_iter)]
        for blk in pending:
            finish(*blk)
        return carry

    lax.fori_loop(0, rows // rows_per_iter, body, 0)


def _nbr_call(q, k, v, bias_tab):
    b, seq, _ = q.shape
    rows = seq // GRID_W
    pairs = D_COLS // LANES
    kern = functools.partial(_nbr_kernel, rows=rows)
    blk = lambda i, p: (i, 0, p)
    return pl.pallas_call(
        kern, grid=(b, pairs),
        in_specs=[pl.BlockSpec((1, seq, LANES), blk), pl.BlockSpec((1, seq, LANES), blk),
                  pl.BlockSpec((1, seq, LANES), blk),
                  pl.BlockSpec((2,) + bias_tab.shape[1:], lambda i, p: (p, 0, 0, 0))],
        out_specs=pl.BlockSpec((1, seq, LANES), blk),
        out_shape=jax.ShapeDtypeStruct((b, seq, D_COLS), BF16),
        compiler_params=pltpu.CompilerParams(dimension_semantics=("arbitrary",) * 2),
        name="mixer_d",
    )(q, k, v, bias_tab)


def _nbr_bias_table(rpb):
    cj = np.arange(GRID_W)
    c0 = np.clip(cj - NA_COLS // 2, 0, GRID_W - NA_COLS)
    col_ok = (cj[None, :] >= c0[:, None]) & (cj[None, :] < c0[:, None] + NA_COLS)
    dc = np.clip(cj[None, :] - cj[:, None], -(NA_COLS - 1), NA_COLS - 1) + (NA_COLS - 1)
    per_row = jnp.where(col_ok[None, None], rpb.astype(F32)[:, :, dc], NEG_INF)
    return jnp.concatenate([per_row[:, :-1], per_row[:, 1:]], axis=-1)


def _merge_mlp_kernel(x_ref, ya_ref, ob1_ref, ob2_ref, ob3_ref, lb1_ref, lb2_ref, lb3_ref, yc_ref, yd_ref,
                      ga_ref, gm_ref, wg_ref, wa_ref, wb_ref, wc_ref, wd_ref, wo_ref, wu_ref, wdn_ref, o_ref):
    x = x_ref[...]
    h = _rms_rows(x, ga_ref[...]).astype(BF16)
    l1, l2, l3 = lb1_ref[...], lb2_ref[...], lb3_ref[...]
    m = jnp.maximum(jnp.maximum(l1, l2), l3)
    w1, w2, w3 = jnp.exp(l1 - m), jnp.exp(l2 - m), jnp.exp(l3 - m)
    yb = (w1 * ob1_ref[...] + w2 * ob2_ref[...] + w3 * ob3_ref[...]) / (w1 + w2 + w3)
    branches = (ya_ref[...], yb.astype(BF16), yc_ref[...], yd_ref[...])
    weights = (wa_ref, wb_ref, wc_ref, wd_ref)
    merged = None
    for c in range(N_BRANCHES):
        z = _dot(h, wg_ref[:, c * D_MODEL:(c + 1) * D_MODEL])
        term = _dot(branches[c], weights[c][...]) / (1.0 + jnp.exp(-z))
        merged = term if merged is None else merged + term
    x1 = x + _dot(merged.astype(BF16), wo_ref[...])
    hm = _rms_rows(x1, gm_ref[...]).astype(BF16)
    acc = x1
    for c in range(D_MLP // D_MODEL):
        u = _dot(hm, wu_ref[:, c * D_MODEL:(c + 1) * D_MODEL])
        u = jnp.square(jnp.maximum(u, 0.0)).astype(BF16)
        acc = acc + _dot(u, wdn_ref[c * D_MODEL:(c + 1) * D_MODEL, :])
    o_ref[...] = acc


def _merge_mlp_call(x2d, ya, obs, lbs, yc, yd, ga, gm, wg, wa, wb, wc, wd, wo, wu, wdn, tm):
    t_tokens = x2d.shape[0]
    row = lambda i: (i, 0)
    const = lambda i: (0, 0)
    single = pl.Buffered(1)
    acts = [x2d, ya, *obs, *lbs, yc, yd]
    consts = [ga, gm, wg, wa, wb, wc, wd, wo, wu, wdn]
    in_specs = ([pl.BlockSpec((tm, a.shape[1]), row) for a in acts]
                + [pl.BlockSpec(w.shape, const, pipeline_mode=single) for w in consts])
    return pl.pallas_call(
        _merge_mlp_kernel, grid=(t_tokens // tm,), in_specs=in_specs,
        out_specs=pl.BlockSpec((tm, D_MODEL), row),
        out_shape=jax.ShapeDtypeStruct((t_tokens, D_MODEL), F32),
        compiler_params=pltpu.CompilerParams(dimension_semantics=("arbitrary",), vmem_limit_bytes=VMEM_LIMIT),
        name="merge_mlp",
    )(*acts, *consts)


def _rope_lane_tables(seq, dim):
    inv = ROPE_THETA ** (-jnp.arange(0, dim, 2, dtype=F32) / dim)
    ang = jnp.arange(seq, dtype=F32)[:, None] * inv[None, :]
    cos, sin = jnp.cos(ang), jnp.sin(ang)
    reps = MXU_COLS // dim
    return (jnp.tile(jnp.concatenate([cos, cos], axis=-1), (1, reps)),
            jnp.tile(jnp.concatenate([-sin, sin], axis=-1), (1, reps)))


def _blockdiag_ones(seg):
    idx = np.arange(MXU_COLS) // seg
    return jnp.asarray(idx[:, None] == idx[None, :], dtype=BF16)


def _gain_row(l, a_g, b_g, c_g, d_g):
    ones = lambda n: jnp.ones((n,), F32)
    s64, s32 = HEAD_DIM ** -0.5, C_QK_DIM ** -0.5
    parts = [
        jnp.tile(a_g[l, 0] * (s64 * LOG2E), A_Q_HEADS), jnp.tile(a_g[l, 1], A_KV_HEADS), ones(A_KV_COLS),
        jnp.tile(b_g[l, 0] * s64, B_HEADS), jnp.tile(b_g[l, 1], B_HEADS), ones(B_COLS),
        jnp.tile(c_g[l, 0] * (s32 * LOG2E), 2 * C_HEADS), jnp.tile(c_g[l, 1], 2 * C_HEADS), ones(C_V_COLS),
        jnp.tile(d_g[l, 0] * s64, D_HEADS), jnp.tile(d_g[l, 1], D_HEADS), ones(D_COLS),
    ]
    return jnp.concatenate(parts).astype(F32)[None, :]


def kernel(x, attn_norm_g, w_in, a_qk_norm_g, a_sink, b_qk_norm_g, c_qk_norm_g, c_lambda, c_subln_g,
           d_qk_norm_g, d_rel_bias, w_branch_a, w_branch_b, w_branch_c, w_branch_d, w_out, mlp_norm_g,
           w_up, w_down):
    b, seq, _ = x.shape
    depth = w_in.shape[0]
    tokens = b * seq
    tm = 256
    tq_c = 256
    cos64, sin64 = _rope_lane_tables(seq, HEAD_DIM)
    cos32, sin32 = _rope_lane_tables(seq, C_QK_DIM)
    tabs = dict(cos64=cos64, sin64=sin64, cos32=cos32, sin32=sin32,
                e64=_blockdiag_ones(HEAD_DIM), e32=_blockdiag_ones(C_QK_DIM))
    xf = x.reshape(tokens, D_MODEL)
    for l in range(depth):
        w_l = w_in[l].astype(BF16)
        gains = _gain_row(l, a_qk_norm_g, b_qk_norm_g, c_qk_norm_g, d_qk_norm_g)
        w_aq = w_l[:, :A_Q_COLS].reshape(D_MODEL, A_Q_HEADS, HEAD_DIM)[:, A_HEAD_ORDER, :].reshape(D_MODEL, A_Q_COLS)
        wqkv = jnp.concatenate([w_aq, w_l[:, A_Q_COLS:QKV_COLS]], axis=1)
        wt = jnp.concatenate([w_l[:, _SECTION_COL[n]:_SECTION_COL[n] + dict((s[0], s[1]) for s in _SECTIONS)[n]].T
                              for n in _TRANSPOSED], axis=0)
        w_ba = (w_branch_a[l].astype(BF16).reshape(A_Q_HEADS, HEAD_DIM, D_MODEL)[A_HEAD_ORDER, :, :]
                .reshape(A_Q_COLS, D_MODEL))
        outs = _proj_call(xf, attn_norm_g[l][None, :], wqkv, wt, gains, tabs, seq, tm)
        aq, ak, avt, bq, bk, bv, cq, ck, cvt, dq, dk, dv = outs
        tok3 = lambda t: t.reshape(b, seq, t.shape[-1])
        ya = _window_call(a_sink[l].astype(F32), tok3(aq), tok3(ak), avt, tq=128)
        obs, lbs = [], []
        for gi, (window, dil) in enumerate(B_PAIRS):
            o, lse = _dilated_call(tok3(bq), tok3(bk), tok3(bv), gi, window, dil, tq=128)
            obs.append(o.reshape(tokens, LANES))
            lbs.append(lse.reshape(tokens, LANES))
        lambda_init = 0.8 - 0.6 * math.exp(-0.3 * l)
        subg = jnp.broadcast_to(jnp.tile(c_subln_g[l].astype(F32), LANES // C_V_DIM)[:, None], (LANES, tq_c))
        yc = _diff_call(tok3(cq), tok3(ck), cvt, c_lambda[l].astype(F32), subg, lambda_init, tq=tq_c, tk=512)
        yd = _nbr_call(tok3(dq), tok3(dk), tok3(dv), _nbr_bias_table(d_rel_bias[l]))
        xf = _merge_mlp_call(xf, ya.reshape(tokens, A_Q_COLS), obs, lbs, yc.reshape(tokens, C_V_COLS),
                             yd.reshape(tokens, D_COLS), attn_norm_g[l][None, :], mlp_norm_g[l][None, :],
                             w_l[:, QKV_COLS:], w_ba, w_branch_b[l].astype(BF16), w_branch_c[l].astype(BF16),
                             w_branch_d[l].astype(BF16), w_out[l].astype(BF16), w_up[l].astype(BF16),
                             w_down[l].astype(BF16), tm)
    return xf.reshape(b, seq, D_MODEL)
```

```python
import functools
import math

import jax
import jax.numpy as jnp
import numpy as np
from jax import lax
from jax.experimental import pallas as pl
from jax.experimental.pallas import tpu as pltpu

F32 = jnp.float32
BF16 = jnp.bfloat16

D_MODEL = 1024
HEAD_DIM = 64
GRID_W = 64
ROPE_THETA = 10000.0
NORM_EPS = 1e-6
NEG_INF = -1e30

A_Q_HEADS = 8
A_KV_HEADS = 2
A_RADIUS = 128
B_PAIRS = ((128, 1), (512, 4), (2048, 16))
B_HEADS_PER_GROUP = 2
B_HEADS = B_HEADS_PER_GROUP * len(B_PAIRS)
C_HEADS = 4
C_QK_DIM = 32
C_V_DIM = 2 * C_QK_DIM
D_HEADS = 4
NA_ROWS = 8
NA_COLS = 16
D_MLP = 4 * D_MODEL
N_BRANCHES = 4

A_Q_COLS = A_Q_HEADS * HEAD_DIM
A_KV_COLS = A_KV_HEADS * HEAD_DIM
B_COLS = B_HEADS * HEAD_DIM
C_QK_COLS = C_HEADS * 2 * C_QK_DIM
C_V_COLS = C_HEADS * C_V_DIM
D_COLS = D_HEADS * HEAD_DIM
GATE_COLS = N_BRANCHES * D_MODEL
QKV_COLS = (A_Q_COLS + 2 * A_KV_COLS + 3 * B_COLS + 2 * C_QK_COLS + C_V_COLS + 3 * D_COLS)

LANES = 128
MXU_COLS = 256
BF16_ROWS = 16
VMEM_LIMIT = 48 * 1024 * 1024
TILES = dict(proj_rows=512, merge_rows=256, a_queries=128, a_tiles_per_step=4, b_queries=128,
             c_queries=512, c_keys=512)

_SECTIONS = (
    ("aq", A_Q_COLS, HEAD_DIM, True, BF16), ("ak", A_KV_COLS, HEAD_DIM, True, BF16), ("av", A_KV_COLS, None, False, BF16),
    ("bq", B_COLS, HEAD_DIM, True, F32), ("bk", B_COLS, HEAD_DIM, True, F32), ("bv", B_COLS, None, False, F32),
    ("cq", C_QK_COLS, C_QK_DIM, True, BF16), ("ck", C_QK_COLS, C_QK_DIM, True, BF16), ("cv", C_V_COLS, None, False, BF16),
    ("dq", D_COLS, HEAD_DIM, False, BF16), ("dk", D_COLS, HEAD_DIM, False, BF16), ("dv", D_COLS, None, False, BF16),
)
_SECTION_COL = {s[0]: sum(t[1] for t in _SECTIONS[:i]) for i, s in enumerate(_SECTIONS)}
_TRANSPOSED = ("av", "cv")
_TRANSPOSED_ROW = {n: sum(dict((s[0], s[1]) for s in _SECTIONS)[m] for m in _TRANSPOSED[:i])
                   for i, n in enumerate(_TRANSPOSED)}
_TRANSPOSED_ROWS = sum(s[1] for s in _SECTIONS if s[0] in _TRANSPOSED)
A_HEAD_ORDER = np.array([h for c in range(A_Q_HEADS // A_KV_HEADS) for h in (c, c + A_Q_HEADS // A_KV_HEADS)])
LOG2E = math.log2(math.e)


def _nt_dot(a, b):
    return lax.dot_general(a, b, (((1,), (1,)), ((), ())), preferred_element_type=F32)


def _dot(a, b):
    return jnp.dot(a, b, preferred_element_type=F32)


def _rms_rows(x, g):
    ms = jnp.mean(x * x, axis=-1, keepdims=True)
    return x * lax.rsqrt(ms + NORM_EPS) * g


def _rotate_half(y, half):
    width = y.shape[1]
    lane = lax.broadcasted_iota(jnp.int32, y.shape, 1)
    first = (lane & (2 * half - 1)) < half
    return jnp.where(first, pltpu.roll(y, width - half, 1), pltpu.roll(y, half, 1))


def _proj_kernel(x_ref, g_ref, wqkv_ref, wt_ref, gains_ref, cos64_ref, sin64_ref, cos32_ref, sin32_ref,
                 e64_ref, e32_ref, *out_refs):
    x = x_ref[...]
    h = _rms_rows(x, g_ref[...]).astype(BF16)

    def project(idx):
        name, width = _SECTIONS[idx][:2]
        if name in _TRANSPOSED:
            r0 = _TRANSPOSED_ROW[name]
            return _nt_dot(wt_ref[r0:r0 + width, :], h)
        return _dot(h, wqkv_ref[:, _SECTION_COL[name]:_SECTION_COL[name] + width])

    t_next = project(0)
    for idx, (name, width, hd, rope, dtype) in enumerate(_SECTIONS):
        o_ref = out_refs[idx]
        col = _SECTION_COL[name]
        t = t_next
        if idx + 1 < len(_SECTIONS):
            t_next = project(idx + 1)
        if hd is None:
            o_ref[...] = t.astype(dtype)
        else:
            e_ref = e64_ref if hd == HEAD_DIM else e32_ref
            cos_ref, sin_ref = (cos64_ref, sin64_ref) if hd == HEAD_DIM else (cos32_ref, sin32_ref)
            for off in range(0, width, MXU_COLS):
                pw = min(MXU_COLS, width - off)
                tc = t[:, off:off + pw]
                ss = _dot((tc * tc).astype(BF16), e_ref[:pw, :pw])
                gain = gains_ref[:, col + off:col + off + pw]
                y = tc * lax.rsqrt(ss * (1.0 / hd) + NORM_EPS) * gain
                if rope:
                    y = y * cos_ref[:, :pw] + _rotate_half(y, hd // 2) * sin_ref[:, :pw]
                o_ref[:, off:off + pw] = y.astype(dtype)


def _proj_call(x2d, g, wqkv, wt, gains, tabs, seq, tm):
    t_tokens = x2d.shape[0]
    n_seq_tiles = seq // tm
    const = lambda i: (0, 0)
    row = lambda i: (i, 0)
    pos = lambda i: (i % n_seq_tiles, 0)
    single = pl.Buffered(1)
    in_specs = [
        pl.BlockSpec((tm, D_MODEL), row),
        pl.BlockSpec((1, D_MODEL), const),
        pl.BlockSpec((D_MODEL, QKV_COLS), const, pipeline_mode=single),
        pl.BlockSpec((_TRANSPOSED_ROWS, D_MODEL), const, pipeline_mode=single),
        pl.BlockSpec((1, QKV_COLS), const),
        pl.BlockSpec((tm, MXU_COLS), pos), pl.BlockSpec((tm, MXU_COLS), pos),
        pl.BlockSpec((tm, MXU_COLS), pos), pl.BlockSpec((tm, MXU_COLS), pos),
        pl.BlockSpec((MXU_COLS, MXU_COLS), const), pl.BlockSpec((MXU_COLS, MXU_COLS), const),
    ]
    out_shape, out_specs = [], []
    for name, w, _, _, dtype in _SECTIONS:
        if name in _TRANSPOSED:
            out_shape.append(jax.ShapeDtypeStruct((w, t_tokens), dtype))
            out_specs.append(pl.BlockSpec((w, tm), lambda i: (0, i)))
        else:
            out_shape.append(jax.ShapeDtypeStruct((t_tokens, w), dtype))
            out_specs.append(pl.BlockSpec((tm, w), row))
    return pl.pallas_call(
        _proj_kernel, grid=(t_tokens // tm,), in_specs=in_specs, out_specs=out_specs, out_shape=out_shape,
        compiler_params=pltpu.CompilerParams(dimension_semantics=("arbitrary",), vmem_limit_bytes=VMEM_LIMIT),
        name="proj",
    )(x2d, g, wqkv, wt, gains, tabs["cos64"], tabs["sin64"], tabs["cos32"], tabs["sin32"],
      tabs["e64"], tabs["e32"])


def _band_bias(n, tq, win, radius, start):
    qpos = n * tq + lax.broadcasted_iota(jnp.int32, (tq, win), 0)
    kpos = start + lax.broadcasted_iota(jnp.int32, (tq, win), 1)
    return jnp.where(jnp.abs(kpos - qpos) <= radius, 0.0, NEG_INF).astype(F32)


def _window_kernel(sink_ref, q_ref, k_ref, vt_ref, o_ref, *, tq, tiles, seq, radius):
    win = tq + 2 * radius
    left = lax.broadcasted_iota(jnp.int32, (tq, LANES), 1) < HEAD_DIM
    first = lax.broadcasted_iota(jnp.int32, (1, 2 * tq), 1) < tq
    rep = A_Q_HEADS // A_KV_HEADS
    ones = jnp.ones((BF16_ROWS, win), BF16)

    def window(sub):
        n = pl.program_id(1) * tiles + sub
        start = pl.multiple_of(jnp.clip(n * tq - radius, 0, seq - win), radius)
        kpos = start + lax.broadcasted_iota(jnp.int32, (win, tq), 0)
        qpos = n * tq + lax.broadcasted_iota(jnp.int32, (win, tq), 1)
        bias = jnp.where(jnp.abs(kpos - qpos) <= radius, 0.0, NEG_INF).astype(F32)
        return (k_ref[0, pl.ds(start, win), :], jnp.concatenate([vt_ref[:, pl.ds(start, win)], ones], axis=0),
                jnp.concatenate([bias, bias], axis=1))

    windows = [window(sub) for sub in range(tiles)]
    jobs = [(sub, c) for sub in range(tiles) for c in range(rep)]

    def scores(job):
        sub, c = job
        qc = q_ref[0, sub * tq:(sub + 1) * tq, c * LANES:(c + 1) * LANES]
        zero = jnp.zeros_like(qc)
        qm = jnp.concatenate([jnp.where(left, qc, zero), jnp.where(left, zero, qc)], axis=0)
        return _nt_dot(windows[sub][0], qm) + windows[sub][2]

    ahead = 4
    pending = [scores(job) for job in jobs[:ahead]]
    for j, (sub, c) in enumerate(jobs):
        st = pending.pop(0)
        if j + ahead < len(jobs):
            pending.append(scores(jobs[j + ahead]))
        sk = jnp.where(first, sink_ref[c], sink_ref[c + rep]) * LOG2E
        m = jnp.maximum(jnp.max(st, axis=0, keepdims=True), sk)
        pt = jnp.exp2(st - m).astype(BF16)
        acc = _dot(windows[sub][1], pt)
        den = acc[LANES:LANES + 1] + jnp.exp2(sk - m)
        ot = jnp.concatenate([acc[:HEAD_DIM, :tq] / den[:, :tq], acc[HEAD_DIM:LANES, tq:] / den[:, tq:]], axis=0)
        o_ref[0, sub * tq:(sub + 1) * tq, c * LANES:(c + 1) * LANES] = ot.T.astype(BF16)


def _window_call(sink, q, k, vt, tq, tiles):
    b, seq, _ = q.shape
    kern = functools.partial(_window_kernel, tq=tq, tiles=tiles, seq=seq, radius=A_RADIUS)
    return pl.pallas_call(
        kern, grid=(b, seq // (tq * tiles)),
        in_specs=[pl.BlockSpec(memory_space=pltpu.SMEM),
                  pl.BlockSpec((1, tq * tiles, A_Q_COLS), lambda i, n: (i, n, 0)),
                  pl.BlockSpec((1, seq, A_KV_COLS), lambda i, n: (i, 0, 0)),
                  pl.BlockSpec((A_KV_COLS, seq), lambda i, n: (0, i))],
        out_specs=pl.BlockSpec((1, tq * tiles, A_Q_COLS), lambda i, n: (i, n, 0)),
        out_shape=jax.ShapeDtypeStruct((b, seq, A_Q_COLS), BF16),
        compiler_params=pltpu.CompilerParams(dimension_semantics=("arbitrary", "arbitrary")),
        name="mixer_a",
    )(sink, q, k, vt)


def _dilated_kernel(q_ref, k_ref, v_ref, o_ref, lse_ref, *, dil, tq, seq, radius):
    sub_len = seq // dil
    n_tiles = sub_len // tq
    win = tq + 2 * radius

    def rows(first, count):
        if dil == 1:
            return pl.ds(pl.multiple_of(first, radius), count)
        return pl.ds(first, count, stride=dil)

    left = lax.broadcasted_iota(jnp.int32, (tq, LANES), 1) < HEAD_DIM
    ones = jnp.ones((win, LANES), BF16)

    blocks_per_iter = 4
    shared_window = win == sub_len and blocks_per_iter % n_tiles == 0

    def load_kv(r, start):
        kv_rows = rows(r + dil * start, win)
        return k_ref[0, kv_rows, :].astype(BF16), v_ref[0, kv_rows, :].astype(BF16)

    def scores(it, kv=None):
        r = it // n_tiles
        n = it - r * n_tiles
        start = jnp.clip(n * tq - radius, 0, sub_len - win)
        q_rows = rows(r + dil * (n * tq), tq)
        q = q_ref[0, q_rows, :].astype(BF16)
        k, v = load_kv(r, start) if kv is None else kv
        zero = jnp.zeros_like(q)
        q2 = jnp.concatenate([jnp.where(left, q, zero), jnp.where(left, zero, q)], axis=0)
        bias = _band_bias(n, tq, win, radius, start)
        return q_rows, v, _nt_dot(q2, k) + jnp.concatenate([bias, bias], axis=0)

    def finish(q_rows, v, s):
        m = jnp.max(s, axis=-1, keepdims=True)
        p = jnp.exp(s - m).astype(BF16)
        acc = _dot(p, jnp.concatenate([v, ones], axis=1))
        den = acc[:, LANES:]
        o2 = acc[:, :LANES] / den
        lse2 = jnp.log(den) + m
        o_ref[0, q_rows, :] = jnp.where(left, o2[:tq], o2[tq:])
        lse_ref[0, q_rows, :] = jnp.where(left, lse2[:tq], lse2[tq:])

    def body(j, carry):
        pending = []
        for u in range(blocks_per_iter):
            it = blocks_per_iter * j + u
            if shared_window and u % n_tiles == 0:
                kv = load_kv(it // n_tiles, 0)
            pending.append(scores(it, kv if shared_window else None))
        for blk in pending:
            finish(*blk)
        return carry

    lax.fori_loop(0, dil * n_tiles // blocks_per_iter, body, 0)


def _dilated_call(q, k, v, group, window, dil, tq):
    b, seq, _ = q.shape
    radius = window // (2 * dil)
    kern = functools.partial(_dilated_kernel, dil=dil, tq=tq, seq=seq, radius=radius)
    out_sds = jax.ShapeDtypeStruct((b, seq, LANES), F32)
    chunk = lambda i: (i, 0, group)
    whole = lambda i: (i, 0, 0)
    return pl.pallas_call(
        kern, grid=(b,),
        in_specs=[pl.BlockSpec((1, seq, LANES), chunk)] * 3,
        out_specs=[pl.BlockSpec((1, seq, LANES), whole)] * 2,
        out_shape=[out_sds, out_sds],
        compiler_params=pltpu.CompilerParams(dimension_semantics=("arbitrary",)),
        name=f"mixer_b_d{dil}",
    )(q, k, v)


def _diff_kernel(q_ref, k_ref, vt_ref, lam_ref, subg_ref, o_ref, *, tq, tk, seq, lambda_init):
    nseg = LANES // C_QK_DIM
    q = q_ref[0]
    lane = lax.broadcasted_iota(jnp.int32, (tq, LANES), 1)
    zero = jnp.zeros_like(q)
    q4 = jnp.concatenate([jnp.where((lane // C_QK_DIM) == j, q, zero) for j in range(nseg)], axis=0)
    ones_rows = jnp.ones((BF16_ROWS, tk), BF16)
    m = jnp.full((1, nseg * tq), NEG_INF, F32)
    acc = jnp.zeros((LANES + BF16_ROWS, nseg * tq), F32)
    n_blocks = seq // tk
    scores = lambda t: _nt_dot(k_ref[0, t * tk:(t + 1) * tk, :], q4)
    ahead = 2
    pending = [scores(t) for t in range(min(ahead, n_blocks))]
    for t in range(n_blocks):
        st = pending.pop(0)
        if t + ahead < n_blocks:
            pending.append(scores(t + ahead))
        m_new = jnp.maximum(m, jnp.max(st, axis=0, keepdims=True))
        pt = jnp.exp2(st - m_new).astype(BF16)
        alpha = jnp.exp2(m - m_new)
        vext = jnp.concatenate([vt_ref[:, t * tk:(t + 1) * tk], ones_rows], axis=0)
        acc = acc * alpha + _dot(vext, pt)
        m = m_new

    lp = lam_ref[...]
    lam = (jnp.exp(jnp.sum(lp[0:1] * lp[1:2], axis=-1, keepdims=True))
           - jnp.exp(jnp.sum(lp[2:3] * lp[3:4], axis=-1, keepdims=True)) + lambda_init)
    heads = []
    for hh in range(LANES // C_V_DIM):
        u = []
        for c in range(2):
            j = 2 * hh + c
            cols = slice(j * tq, (j + 1) * tq)
            u.append(acc[hh * C_V_DIM:(hh + 1) * C_V_DIM, cols] / acc[LANES:LANES + 1, cols])
        o = u[0] - lam * u[1]
        ss = jnp.sum(o * o, axis=0, keepdims=True)
        heads.append(o * lax.rsqrt(ss * (1.0 / C_V_DIM) + NORM_EPS))
    ot = jnp.concatenate(heads, axis=0) * subg_ref[...] * (1.0 - lambda_init)
    o_ref[0] = ot.T.astype(BF16)


def _diff_call(q, k, vt, lam_p, subg, lambda_init, tq, tk):
    b, seq, _ = q.shape
    pairs = C_QK_COLS // LANES
    kern = functools.partial(_diff_kernel, tq=tq, tk=tk, seq=seq, lambda_init=lambda_init)
    return pl.pallas_call(
        kern, grid=(b, pairs, seq // tq),
        in_specs=[pl.BlockSpec((1, tq, LANES), lambda i, p, n: (i, n, p)),
                  pl.BlockSpec((1, seq, LANES), lambda i, p, n: (i, 0, p)),
                  pl.BlockSpec((LANES, seq), lambda i, p, n: (p, i)),
                  pl.BlockSpec((4, C_QK_DIM), lambda i, p, n: (0, 0)),
                  pl.BlockSpec((LANES, tq), lambda i, p, n: (0, 0))],
        out_specs=pl.BlockSpec((1, tq, LANES), lambda i, p, n: (i, n, p)),
        out_shape=jax.ShapeDtypeStruct((b, seq, C_V_COLS), BF16),
        compiler_params=pltpu.CompilerParams(dimension_semantics=("arbitrary",) * 3),
        name="mixer_c",
    )(q, k, vt, lam_p, subg)


def _nbr_kernel(q_ref, k_ref, v_ref, bias_ref, o_ref, *, rows):
    kr = NA_ROWS
    lane = lax.broadcasted_iota(jnp.int32, (GRID_W, LANES), 1)
    left = lane < HEAD_DIM

    ones = jnp.ones((kr * GRID_W, LANES), BF16)

    def scores(i):
        r0 = jnp.clip(i - kr // 2, 0, rows - kr)
        cls = jnp.minimum(i, kr // 2) + jnp.maximum(i - (rows - kr // 2), 0)
        q = q_ref[0, pl.ds(pl.multiple_of(i * GRID_W, GRID_W), GRID_W), :]
        koff = pl.multiple_of(r0 * GRID_W, GRID_W)
        k = k_ref[0, pl.ds(koff, kr * GRID_W), :]
        v = v_ref[0, pl.ds(koff, kr * GRID_W), :]
        zero = jnp.zeros_like(q)
        q2 = jnp.concatenate([jnp.where(left, q, zero), jnp.where(left, zero, q)], axis=0)
        bias = jnp.concatenate(
            [jnp.concatenate([bias_ref[hh, NA_ROWS - 1 - cls + 2 * u] for u in range(kr // 2)], axis=1)
             for hh in range(2)], axis=0)
        return i, v, _nt_dot(q2, k) + bias

    def finish(i, v, s):
        m = jnp.max(s, axis=-1, keepdims=True)
        p = jnp.exp(s - m).astype(BF16)
        acc = _dot(p, jnp.concatenate([v, ones], axis=1))
        pv = acc[:, :LANES] / acc[:, LANES:]
        o = jnp.where(left, pv[:GRID_W], pv[GRID_W:])
        o_ref[0, pl.ds(pl.multiple_of(i * GRID_W, GRID_W), GRID_W), :] = o.astype(BF16)

    rows_per_iter = 8

    def body(j, carry):
        pending = [scores(rows_per_iter * j + u) for u in range(rows_per_iter)]
        for blk in pending:
            finish(*blk)
        return carry

    lax.fori_loop(0, rows // rows_per_iter, body, 0)


def _nbr_call(q, k, v, bias_tab):
    b, seq, _ = q.shape
    rows = seq // GRID_W
    pairs = D_COLS // LANES
    kern = functools.partial(_nbr_kernel, rows=rows)
    blk = lambda i, p: (i, 0, p)
    return pl.pallas_call(
        kern, grid=(b, pairs),
        in_specs=[pl.BlockSpec((1, seq, LANES), blk), pl.BlockSpec((1, seq, LANES), blk),
                  pl.BlockSpec((1, seq, LANES), blk),
                  pl.BlockSpec((2,) + bias_tab.shape[1:], lambda i, p: (p, 0, 0, 0))],
        out_specs=pl.BlockSpec((1, seq, LANES), blk),
        out_shape=jax.ShapeDtypeStruct((b, seq, D_COLS), BF16),
        compiler_params=pltpu.CompilerParams(dimension_semantics=("arbitrary",) * 2),
        name="mixer_d",
    )(q, k, v, bias_tab)


def _nbr_bias_table(rpb):
    cj = np.arange(GRID_W)
    c0 = np.clip(cj - NA_COLS // 2, 0, GRID_W - NA_COLS)
    col_ok = (cj[None, :] >= c0[:, None]) & (cj[None, :] < c0[:, None] + NA_COLS)
    dc = np.clip(cj[None, :] - cj[:, None], -(NA_COLS - 1), NA_COLS - 1) + (NA_COLS - 1)
    onehot = jnp.asarray(np.arange(2 * NA_COLS - 1)[:, None, None] == dc[None], dtype=F32)
    per_row = jnp.einsum('hrj,jqk->hrqk', rpb.astype(F32), onehot, precision=lax.Precision.HIGHEST)
    per_row = jnp.where(col_ok[None, None], per_row, NEG_INF)
    return jnp.concatenate([per_row[:, :-1], per_row[:, 1:]], axis=-1)


def _merge_mlp_kernel(x_ref, ya_ref, ob1_ref, ob2_ref, ob3_ref, lb1_ref, lb2_ref, lb3_ref, yc_ref, yd_ref,
                      ga_ref, gm_ref, wg_ref, wa_ref, wb_ref, wc_ref, wd_ref, wo_ref, wu_ref, wdn_ref, o_ref):
    x = x_ref[...]
    h = _rms_rows(x, ga_ref[...]).astype(BF16)
    l1, l2, l3 = lb1_ref[...], lb2_ref[...], lb3_ref[...]
    m = jnp.maximum(jnp.maximum(l1, l2), l3)
    w1, w2, w3 = jnp.exp(l1 - m), jnp.exp(l2 - m), jnp.exp(l3 - m)
    yb = (w1 * ob1_ref[...] + w2 * ob2_ref[...] + w3 * ob3_ref[...]) / (w1 + w2 + w3)
    branches = (ya_ref[...], yb.astype(BF16), yc_ref[...], yd_ref[...])
    weights = (wa_ref, wb_ref, wc_ref, wd_ref)
    merged = None
    for c in range(N_BRANCHES):
        z = _dot(h, wg_ref[:, c * D_MODEL:(c + 1) * D_MODEL])
        term = _dot(branches[c], weights[c][...]) / (1.0 + jnp.exp(-z))
        merged = term if merged is None else merged + term
    x1 = x + _dot(merged.astype(BF16), wo_ref[...])
    hm = _rms_rows(x1, gm_ref[...]).astype(BF16)
    acc = x1
    n_chunks = D_MLP // D_MODEL
    up = lambda c: _dot(hm, wu_ref[:, c * D_MODEL:(c + 1) * D_MODEL])
    u_next = up(0)
    for c in range(n_chunks):
        u = u_next
        if c + 1 < n_chunks:
            u_next = up(c + 1)
        u = jnp.square(jnp.maximum(u, 0.0)).astype(BF16)
        acc = acc + _dot(u, wdn_ref[c * D_MODEL:(c + 1) * D_MODEL, :])
    o_ref[...] = acc


def _merge_mlp_call(x2d, ya, obs, lbs, yc, yd, ga, gm, wg, wa, wb, wc, wd, wo, wu, wdn, tm):
    t_tokens = x2d.shape[0]
    row = lambda i: (i, 0)
    const = lambda i: (0, 0)
    single = pl.Buffered(1)
    acts = [x2d, ya, *obs, *lbs, yc, yd]
    consts = [ga, gm, wg, wa, wb, wc, wd, wo, wu, wdn]
    in_specs = ([pl.BlockSpec((tm, a.shape[1]), row) for a in acts]
                + [pl.BlockSpec(w.shape, const, pipeline_mode=single) for w in consts])
    return pl.pallas_call(
        _merge_mlp_kernel, grid=(t_tokens // tm,), in_specs=in_specs,
        out_specs=pl.BlockSpec((tm, D_MODEL), row),
        out_shape=jax.ShapeDtypeStruct((t_tokens, D_MODEL), F32),
        compiler_params=pltpu.CompilerParams(dimension_semantics=("arbitrary",), vmem_limit_bytes=VMEM_LIMIT),
        name="merge_mlp",
    )(*acts, *consts)


def _rope_lane_tables(seq, dim):
    inv = ROPE_THETA ** (-jnp.arange(0, dim, 2, dtype=F32) / dim)
    ang = jnp.arange(seq, dtype=F32)[:, None] * inv[None, :]
    cos, sin = jnp.cos(ang), jnp.sin(ang)
    reps = MXU_COLS // dim
    return (jnp.tile(jnp.concatenate([cos, cos], axis=-1), (1, reps)),
            jnp.tile(jnp.concatenate([-sin, sin], axis=-1), (1, reps)))


def _blockdiag_ones(seg):
    idx = np.arange(MXU_COLS) // seg
    return jnp.asarray(idx[:, None] == idx[None, :], dtype=BF16)


def _gain_row(l, a_g, b_g, c_g, d_g):
    ones = lambda n: jnp.ones((n,), F32)
    s64, s32 = HEAD_DIM ** -0.5, C_QK_DIM ** -0.5
    parts = [
        jnp.tile(a_g[l, 0] * (s64 * LOG2E), A_Q_HEADS), jnp.tile(a_g[l, 1], A_KV_HEADS), ones(A_KV_COLS),
        jnp.tile(b_g[l, 0] * s64, B_HEADS), jnp.tile(b_g[l, 1], B_HEADS), ones(B_COLS),
        jnp.tile(c_g[l, 0] * (s32 * LOG2E), 2 * C_HEADS), jnp.tile(c_g[l, 1], 2 * C_HEADS), ones(C_V_COLS),
        jnp.tile(d_g[l, 0] * s64, D_HEADS), jnp.tile(d_g[l, 1], D_HEADS), ones(D_COLS),
    ]
    return jnp.concatenate(parts).astype(F32)[None, :]


def kernel(x, attn_norm_g, w_in, a_qk_norm_g, a_sink, b_qk_norm_g, c_qk_norm_g, c_lambda, c_subln_g,
           d_qk_norm_g, d_rel_bias, w_branch_a, w_branch_b, w_branch_c, w_branch_d, w_out, mlp_norm_g,
           w_up, w_down):
    b, seq, _ = x.shape
    depth = w_in.shape[0]
    tokens = b * seq
    tm, tm_proj, tq_c = TILES["merge_rows"], TILES["proj_rows"], TILES["c_queries"]
    cos64, sin64 = _rope_lane_tables(seq, HEAD_DIM)
    cos32, sin32 = _rope_lane_tables(seq, C_QK_DIM)
    tabs = dict(cos64=cos64, sin64=sin64, cos32=cos32, sin32=sin32,
                e64=_blockdiag_ones(HEAD_DIM), e32=_blockdiag_ones(C_QK_DIM))
    xf = x.reshape(tokens, D_MODEL)
    for l in range(depth):
        w_l = w_in[l].astype(BF16)
        gains = _gain_row(l, a_qk_norm_g, b_qk_norm_g, c_qk_norm_g, d_qk_norm_g)
        w_aq = w_l[:, :A_Q_COLS].reshape(D_MODEL, A_Q_HEADS, HEAD_DIM)[:, A_HEAD_ORDER, :].reshape(D_MODEL, A_Q_COLS)
        wqkv = jnp.concatenate([w_aq, w_l[:, A_Q_COLS:QKV_COLS]], axis=1)
        wt = jnp.concatenate([w_l[:, _SECTION_COL[n]:_SECTION_COL[n] + dict((s[0], s[1]) for s in _SECTIONS)[n]].T
                              for n in _TRANSPOSED], axis=0)
        w_ba = (w_branch_a[l].astype(BF16).reshape(A_Q_HEADS, HEAD_DIM, D_MODEL)[A_HEAD_ORDER, :, :]
                .reshape(A_Q_COLS, D_MODEL))
        outs = _proj_call(xf, attn_norm_g[l][None, :], wqkv, wt, gains, tabs, seq, tm_proj)
        aq, ak, avt, bq, bk, bv, cq, ck, cvt, dq, dk, dv = outs
        tok3 = lambda t: t.reshape(b, seq, t.shape[-1])
        ya = _window_call(a_sink[l].astype(F32), tok3(aq), tok3(ak), avt, tq=TILES["a_queries"],
                          tiles=TILES["a_tiles_per_step"])
        obs, lbs = [], []
        for gi, (window, dil) in enumerate(B_PAIRS):
            o, lse = _dilated_call(tok3(bq), tok3(bk), tok3(bv), gi, window, dil, tq=TILES["b_queries"])
            obs.append(o.reshape(tokens, LANES))
            lbs.append(lse.reshape(tokens, LANES))
        lambda_init = 0.8 - 0.6 * math.exp(-0.3 * l)
        subg = jnp.broadcast_to(jnp.tile(c_subln_g[l].astype(F32), LANES // C_V_DIM)[:, None], (LANES, tq_c))
        yc = _diff_call(tok3(cq), tok3(ck), cvt, c_lambda[l].astype(F32), subg, lambda_init, tq=tq_c,
                        tk=TILES["c_keys"])
        yd = _nbr_call(tok3(dq), tok3(dk), tok3(dv), _nbr_bias_table(d_rel_bias[l]))
        xf = _merge_mlp_call(xf, ya.reshape(tokens, A_Q_COLS), obs, lbs, yc.reshape(tokens, C_V_COLS),
                             yd.reshape(tokens, D_COLS), attn_norm_g[l][None, :], mlp_norm_g[l][None, :],
                             w_l[:, QKV_COLS:], w_ba, w_branch_b[l].astype(BF16), w_branch_c[l].astype(BF16),
                             w_branch_d[l].astype(BF16), w_out[l].astype(BF16), w_up[l].astype(BF16),
                             w_down[l].astype(BF16), tm)
    return xf.reshape(b, seq, D_MODEL)
```

```python
import functools
import math

import jax
import jax.numpy as jnp
import numpy as np
from jax import lax
from jax.experimental import pallas as pl
from jax.experimental.pallas import tpu as pltpu

F32 = jnp.float32
BF16 = jnp.bfloat16

D_MODEL = 1024
HEAD_DIM = 64
GRID_W = 64
ROPE_THETA = 10000.0
NORM_EPS = 1e-6
NEG_INF = -1e30

A_Q_HEADS = 8
A_KV_HEADS = 2
A_RADIUS = 128
B_PAIRS = ((128, 1), (512, 4), (2048, 16))
B_HEADS_PER_GROUP = 2
B_HEADS = B_HEADS_PER_GROUP * len(B_PAIRS)
C_HEADS = 4
C_QK_DIM = 32
C_V_DIM = 2 * C_QK_DIM
D_HEADS = 4
NA_ROWS = 8
NA_COLS = 16
D_MLP = 4 * D_MODEL
N_BRANCHES = 4

A_Q_COLS = A_Q_HEADS * HEAD_DIM
A_KV_COLS = A_KV_HEADS * HEAD_DIM
B_COLS = B_HEADS * HEAD_DIM
C_QK_COLS = C_HEADS * 2 * C_QK_DIM
C_V_COLS = C_HEADS * C_V_DIM
D_COLS = D_HEADS * HEAD_DIM
GATE_COLS = N_BRANCHES * D_MODEL
QKV_COLS = (A_Q_COLS + 2 * A_KV_COLS + 3 * B_COLS + 2 * C_QK_COLS + C_V_COLS + 3 * D_COLS)

LANES = 128
MXU_COLS = 256
BF16_ROWS = 16
VMEM_LIMIT = 48 * 1024 * 1024
TILES = dict(proj_rows=512, merge_rows=256, a_queries=128, a_tiles_per_step=4, b_queries=128,
             c_queries=512, c_keys=256, c_keys_exact=512)

_SECTIONS = (
    ("aq", A_Q_COLS, HEAD_DIM, True, BF16), ("ak", A_KV_COLS, HEAD_DIM, True, BF16), ("av", A_KV_COLS, None, False, BF16),
    ("bq", B_COLS, HEAD_DIM, True, F32), ("bk", B_COLS, HEAD_DIM, True, F32), ("bv", B_COLS, None, False, F32),
    ("cq", C_QK_COLS, C_QK_DIM, True, BF16), ("ck", C_QK_COLS, C_QK_DIM, True, BF16), ("cv", C_V_COLS, None, False, BF16),
    ("dq", D_COLS, HEAD_DIM, False, BF16), ("dk", D_COLS, HEAD_DIM, False, BF16), ("dv", D_COLS, None, False, BF16),
)
_SECTION_COL = {s[0]: sum(t[1] for t in _SECTIONS[:i]) for i, s in enumerate(_SECTIONS)}
_TRANSPOSED = ("av", "cv")
_TRANSPOSED_ROW = {n: sum(dict((s[0], s[1]) for s in _SECTIONS)[m] for m in _TRANSPOSED[:i])
                   for i, n in enumerate(_TRANSPOSED)}
_TRANSPOSED_ROWS = sum(s[1] for s in _SECTIONS if s[0] in _TRANSPOSED)
A_HEAD_ORDER = np.array([h for c in range(A_Q_HEADS // A_KV_HEADS) for h in (c, c + A_Q_HEADS // A_KV_HEADS)])
LOG2E = math.log2(math.e)
C_LAGGED_LOGIT_LIMIT = 16.0


def _nt_dot(a, b):
    return lax.dot_general(a, b, (((1,), (1,)), ((), ())), preferred_element_type=F32)


def _dot(a, b):
    return jnp.dot(a, b, preferred_element_type=F32)


def _rms_rows(x, g):
    ms = jnp.mean(x * x, axis=-1, keepdims=True)
    return x * lax.rsqrt(ms + NORM_EPS) * g


def _rotate_half(y, half):
    width = y.shape[1]
    lane = lax.broadcasted_iota(jnp.int32, y.shape, 1)
    first = (lane & (2 * half - 1)) < half
    return jnp.where(first, pltpu.roll(y, width - half, 1), pltpu.roll(y, half, 1))


def _proj_kernel(x_ref, g_ref, wqkv_ref, wt_ref, gains_ref, cos64_ref, sin64_ref, cos32_ref, sin32_ref,
                 e64_ref, e32_ref, *out_refs):
    x = x_ref[...]
    h = _rms_rows(x, g_ref[...]).astype(BF16)

    def project(idx):
        name, width = _SECTIONS[idx][:2]
        if name in _TRANSPOSED:
            r0 = _TRANSPOSED_ROW[name]
            return _nt_dot(wt_ref[r0:r0 + width, :], h)
        return _dot(h, wqkv_ref[:, _SECTION_COL[name]:_SECTION_COL[name] + width])

    t_next = project(0)
    for idx, (name, width, hd, rope, dtype) in enumerate(_SECTIONS):
        o_ref = out_refs[idx]
        col = _SECTION_COL[name]
        t = t_next
        if idx + 1 < len(_SECTIONS):
            t_next = project(idx + 1)
        if hd is None:
            o_ref[...] = t.astype(dtype)
        else:
            e_ref = e64_ref if hd == HEAD_DIM else e32_ref
            cos_ref, sin_ref = (cos64_ref, sin64_ref) if hd == HEAD_DIM else (cos32_ref, sin32_ref)
            for off in range(0, width, MXU_COLS):
                pw = min(MXU_COLS, width - off)
                tc = t[:, off:off + pw]
                ss = _dot((tc * tc).astype(BF16), e_ref[:pw, :pw])
                gain = gains_ref[:, col + off:col + off + pw]
                y = tc * lax.rsqrt(ss * (1.0 / hd) + NORM_EPS) * gain
                if rope:
                    y = y * cos_ref[:, :pw] + _rotate_half(y, hd // 2) * sin_ref[:, :pw]
                o_ref[:, off:off + pw] = y.astype(dtype)


def _proj_call(x2d, g, wqkv, wt, gains, tabs, seq, tm):
    t_tokens = x2d.shape[0]
    n_seq_tiles = seq // tm
    const = lambda i: (0, 0)
    row = lambda i: (i, 0)
    pos = lambda i: (i % n_seq_tiles, 0)
    single = pl.Buffered(1)
    in_specs = [
        pl.BlockSpec((tm, D_MODEL), row),
        pl.BlockSpec((1, D_MODEL), const),
        pl.BlockSpec((D_MODEL, QKV_COLS), const, pipeline_mode=single),
        pl.BlockSpec((_TRANSPOSED_ROWS, D_MODEL), const, pipeline_mode=single),
        pl.BlockSpec((1, QKV_COLS), const),
        pl.BlockSpec((tm, MXU_COLS), pos), pl.BlockSpec((tm, MXU_COLS), pos),
        pl.BlockSpec((tm, MXU_COLS), pos), pl.BlockSpec((tm, MXU_COLS), pos),
        pl.BlockSpec((MXU_COLS, MXU_COLS), const), pl.BlockSpec((MXU_COLS, MXU_COLS), const),
    ]
    out_shape, out_specs = [], []
    for name, w, _, _, dtype in _SECTIONS:
        if name in _TRANSPOSED:
            out_shape.append(jax.ShapeDtypeStruct((w, t_tokens), dtype))
            out_specs.append(pl.BlockSpec((w, tm), lambda i: (0, i)))
        else:
            out_shape.append(jax.ShapeDtypeStruct((t_tokens, w), dtype))
            out_specs.append(pl.BlockSpec((tm, w), row))
    return pl.pallas_call(
        _proj_kernel, grid=(t_tokens // tm,), in_specs=in_specs, out_specs=out_specs, out_shape=out_shape,
        compiler_params=pltpu.CompilerParams(dimension_semantics=("arbitrary",), vmem_limit_bytes=VMEM_LIMIT),
        name="proj",
    )(x2d, g, wqkv, wt, gains, tabs["cos64"], tabs["sin64"], tabs["cos32"], tabs["sin32"],
      tabs["e64"], tabs["e32"])


def _band_bias(n, tq, win, radius, start):
    qpos = n * tq + lax.broadcasted_iota(jnp.int32, (tq, win), 0)
    kpos = start + lax.broadcasted_iota(jnp.int32, (tq, win), 1)
    return jnp.where(jnp.abs(kpos - qpos) <= radius, 0.0, NEG_INF).astype(F32)


def _window_kernel(sink_ref, q_ref, k_ref, vt_ref, o_ref, *, tq, tiles, seq, radius):
    win = tq + 2 * radius
    left = lax.broadcasted_iota(jnp.int32, (tq, LANES), 1) < HEAD_DIM
    first = lax.broadcasted_iota(jnp.int32, (1, 2 * tq), 1) < tq
    rep = A_Q_HEADS // A_KV_HEADS
    ones = jnp.ones((BF16_ROWS, win), BF16)

    def window(sub):
        n = pl.program_id(1) * tiles + sub
        start = pl.multiple_of(jnp.clip(n * tq - radius, 0, seq - win), radius)
        kpos = start + lax.broadcasted_iota(jnp.int32, (win, tq), 0)
        qpos = n * tq + lax.broadcasted_iota(jnp.int32, (win, tq), 1)
        bias = jnp.where(jnp.abs(kpos - qpos) <= radius, 0.0, NEG_INF).astype(F32)
        return (k_ref[0, pl.ds(start, win), :], jnp.concatenate([vt_ref[:, pl.ds(start, win)], ones], axis=0),
                jnp.concatenate([bias, bias], axis=1))

    windows = [window(sub) for sub in range(tiles)]
    jobs = [(sub, c) for sub in range(tiles) for c in range(rep)]

    def scores(job):
        sub, c = job
        qc = q_ref[0, sub * tq:(sub + 1) * tq, c * LANES:(c + 1) * LANES]
        zero = jnp.zeros_like(qc)
        qm = jnp.concatenate([jnp.where(left, qc, zero), jnp.where(left, zero, qc)], axis=0)
        return _nt_dot(windows[sub][0], qm) + windows[sub][2]

    ahead = 4
    pending = [scores(job) for job in jobs[:ahead]]
    for j, (sub, c) in enumerate(jobs):
        st = pending.pop(0)
        if j + ahead < len(jobs):
            pending.append(scores(jobs[j + ahead]))
        sk = jnp.where(first, sink_ref[c], sink_ref[c + rep]) * LOG2E
        m = jnp.maximum(jnp.max(st, axis=0, keepdims=True), sk)
        pt = jnp.exp2(st - m).astype(BF16)
        acc = _dot(windows[sub][1], pt)
        den = acc[LANES:LANES + 1] + jnp.exp2(sk - m)
        ot = jnp.concatenate([acc[:HEAD_DIM, :tq] / den[:, :tq], acc[HEAD_DIM:LANES, tq:] / den[:, tq:]], axis=0)
        o_ref[0, sub * tq:(sub + 1) * tq, c * LANES:(c + 1) * LANES] = ot.T.astype(BF16)


def _window_call(sink, q, k, vt, tq, tiles):
    b, seq, _ = q.shape
    kern = functools.partial(_window_kernel, tq=tq, tiles=tiles, seq=seq, radius=A_RADIUS)
    return pl.pallas_call(
        kern, grid=(b, seq // (tq * tiles)),
        in_specs=[pl.BlockSpec(memory_space=pltpu.SMEM),
                  pl.BlockSpec((1, tq * tiles, A_Q_COLS), lambda i, n: (i, n, 0)),
                  pl.BlockSpec((1, seq, A_KV_COLS), lambda i, n: (i, 0, 0)),
                  pl.BlockSpec((A_KV_COLS, seq), lambda i, n: (0, i))],
        out_specs=pl.BlockSpec((1, tq * tiles, A_Q_COLS), lambda i, n: (i, n, 0)),
        out_shape=jax.ShapeDtypeStruct((b, seq, A_Q_COLS), BF16),
        compiler_params=pltpu.CompilerParams(dimension_semantics=("arbitrary", "arbitrary")),
        name="mixer_a",
    )(sink, q, k, vt)


def _dilated_kernel(q_ref, k_ref, v_ref, o_ref, lse_ref, *, dil, tq, seq, radius):
    sub_len = seq // dil
    n_tiles = sub_len // tq
    win = tq + 2 * radius

    def rows(first, count):
        if dil == 1:
            return pl.ds(pl.multiple_of(first, radius), count)
        return pl.ds(first, count, stride=dil)

    left = lax.broadcasted_iota(jnp.int32, (tq, LANES), 1) < HEAD_DIM
    ones = jnp.ones((win, LANES), BF16)

    blocks_per_iter = 4
    shared_window = win == sub_len and blocks_per_iter % n_tiles == 0

    def load_kv(r, start):
        kv_rows = rows(r + dil * start, win)
        return k_ref[0, kv_rows, :].astype(BF16), v_ref[0, kv_rows, :].astype(BF16)

    def scores(it, kv=None):
        r = it // n_tiles
        n = it - r * n_tiles
        start = jnp.clip(n * tq - radius, 0, sub_len - win)
        q_rows = rows(r + dil * (n * tq), tq)
        q = q_ref[0, q_rows, :].astype(BF16)
        k, v = load_kv(r, start) if kv is None else kv
        zero = jnp.zeros_like(q)
        q2 = jnp.concatenate([jnp.where(left, q, zero), jnp.where(left, zero, q)], axis=0)
        bias = _band_bias(n, tq, win, radius, start)
        return q_rows, v, _nt_dot(q2, k) + jnp.concatenate([bias, bias], axis=0)

    def finish(q_rows, v, s):
        m = jnp.max(s, axis=-1, keepdims=True)
        p = jnp.exp(s - m).astype(BF16)
        acc = _dot(p, jnp.concatenate([v, ones], axis=1))
        den = acc[:, LANES:]
        o2 = acc[:, :LANES] / den
        lse2 = jnp.log(den) + m
        o_ref[0, q_rows, :] = jnp.where(left, o2[:tq], o2[tq:])
        lse_ref[0, q_rows, :] = jnp.where(left, lse2[:tq], lse2[tq:])

    def body(j, carry):
        pending = []
        for u in range(blocks_per_iter):
            it = blocks_per_iter * j + u
            if shared_window and u % n_tiles == 0:
                kv = load_kv(it // n_tiles, 0)
            pending.append(scores(it, kv if shared_window else None))
        for blk in pending:
            finish(*blk)
        return carry

    lax.fori_loop(0, dil * n_tiles // blocks_per_iter, body, 0)


def _dilated_call(q, k, v, group, window, dil, tq):
    b, seq, _ = q.shape
    radius = window // (2 * dil)
    kern = functools.partial(_dilated_kernel, dil=dil, tq=tq, seq=seq, radius=radius)
    out_sds = jax.ShapeDtypeStruct((b, seq, LANES), F32)
    chunk = lambda i: (i, 0, group)
    whole = lambda i: (i, 0, 0)
    return pl.pallas_call(
        kern, grid=(b,),
        in_specs=[pl.BlockSpec((1, seq, LANES), chunk)] * 3,
        out_specs=[pl.BlockSpec((1, seq, LANES), whole)] * 2,
        out_shape=[out_sds, out_sds],
        compiler_params=pltpu.CompilerParams(dimension_semantics=("arbitrary",)),
        name=f"mixer_b_d{dil}",
    )(q, k, v)


def _diff_kernel(q_ref, k_ref, vt_ref, lam_ref, subg_ref, o_ref, *, tq, tk, seq, lambda_init, lagged):
    nseg = LANES // C_QK_DIM
    q = q_ref[0]
    lane = lax.broadcasted_iota(jnp.int32, (tq, LANES), 1)
    zero = jnp.zeros_like(q)
    q4 = jnp.concatenate([jnp.where((lane // C_QK_DIM) == j, q, zero) for j in range(nseg)], axis=0)
    ones_rows = jnp.ones((BF16_ROWS, tk), BF16)
    m = jnp.full((1, nseg * tq), NEG_INF, F32)
    ref = m
    acc = jnp.zeros((LANES + BF16_ROWS, nseg * tq), F32)
    n_blocks = seq // tk
    scores = lambda t: _nt_dot(k_ref[0, t * tk:(t + 1) * tk, :], q4)
    ahead = 1 if lagged else 2
    pending = [scores(t) for t in range(min(ahead, n_blocks))]
    for t in range(n_blocks):
        st = pending.pop(0)
        if t + ahead < n_blocks:
            pending.append(scores(t + ahead))
        colmax = jnp.max(st, axis=0, keepdims=True)
        ref_new = m if (lagged and t > 0) else jnp.maximum(m, colmax)
        alpha = jnp.exp2(ref - ref_new)
        pt = jnp.exp2(st - ref_new).astype(BF16)
        vext = jnp.concatenate([vt_ref[:, t * tk:(t + 1) * tk], ones_rows], axis=0)
        acc = acc * alpha + _dot(vext, pt)
        ref = ref_new
        m = jnp.maximum(m, colmax)

    lp = lam_ref[...]
    lam = (jnp.exp(jnp.sum(lp[0:1] * lp[1:2], axis=-1, keepdims=True))
           - jnp.exp(jnp.sum(lp[2:3] * lp[3:4], axis=-1, keepdims=True)) + lambda_init)
    heads = []
    for hh in range(LANES // C_V_DIM):
        u = []
        for c in range(2):
            j = 2 * hh + c
            cols = slice(j * tq, (j + 1) * tq)
            u.append(acc[hh * C_V_DIM:(hh + 1) * C_V_DIM, cols] / acc[LANES:LANES + 1, cols])
        o = u[0] - lam * u[1]
        ss = jnp.sum(o * o, axis=0, keepdims=True)
        heads.append(o * lax.rsqrt(ss * (1.0 / C_V_DIM) + NORM_EPS))
    ot = jnp.concatenate(heads, axis=0) * subg_ref[...] * (1.0 - lambda_init)
    o_ref[0] = ot.T.astype(BF16)


def _diff_call(q, k, vt, lam_p, subg, lambda_init, tq, tk, lagged):
    b, seq, _ = q.shape
    pairs = C_QK_COLS // LANES
    kern = functools.partial(_diff_kernel, tq=tq, tk=tk, seq=seq, lambda_init=lambda_init, lagged=lagged)
    return pl.pallas_call(
        kern, grid=(b, pairs, seq // tq),
        in_specs=[pl.BlockSpec((1, tq, LANES), lambda i, p, n: (i, n, p)),
                  pl.BlockSpec((1, seq, LANES), lambda i, p, n: (i, 0, p)),
                  pl.BlockSpec((LANES, seq), lambda i, p, n: (p, i)),
                  pl.BlockSpec((4, C_QK_DIM), lambda i, p, n: (0, 0)),
                  pl.BlockSpec((LANES, tq), lambda i, p, n: (0, 0))],
        out_specs=pl.BlockSpec((1, tq, LANES), lambda i, p, n: (i, n, p)),
        out_shape=jax.ShapeDtypeStruct((b, seq, C_V_COLS), BF16),
        compiler_params=pltpu.CompilerParams(dimension_semantics=("arbitrary",) * 3),
        name="mixer_c_lagged" if lagged else "mixer_c",
    )(q, k, vt, lam_p, subg)


def _nbr_kernel(q_ref, k_ref, v_ref, bias_ref, o_ref, *, rows):
    kr = NA_ROWS
    lane = lax.broadcasted_iota(jnp.int32, (GRID_W, LANES), 1)
    left = lane < HEAD_DIM

    ones = jnp.ones((kr * GRID_W, LANES), BF16)

    def scores(i):
        r0 = jnp.clip(i - kr // 2, 0, rows - kr)
        cls = jnp.minimum(i, kr // 2) + jnp.maximum(i - (rows - kr // 2), 0)
        q = q_ref[0, pl.ds(pl.multiple_of(i * GRID_W, GRID_W), GRID_W), :]
        koff = pl.multiple_of(r0 * GRID_W, GRID_W)
        k = k_ref[0, pl.ds(koff, kr * GRID_W), :]
        v = v_ref[0, pl.ds(koff, kr * GRID_W), :]
        zero = jnp.zeros_like(q)
        q2 = jnp.concatenate([jnp.where(left, q, zero), jnp.where(left, zero, q)], axis=0)
        bias = jnp.concatenate(
            [jnp.concatenate([bias_ref[hh, NA_ROWS - 1 - cls + 2 * u] for u in range(kr // 2)], axis=1)
             for hh in range(2)], axis=0)
        return i, v, _nt_dot(q2, k) + bias

    def finish(i, v, s):
        m = jnp.max(s, axis=-1, keepdims=True)
        p = jnp.exp(s - m).astype(BF16)
        acc = _dot(p, jnp.concatenate([v, ones], axis=1))
        pv = acc[:, :LANES] / acc[:, LANES:]
        o = jnp.where(left, pv[:GRID_W], pv[GRID_W:])
        o_ref[0, pl.ds(pl.multiple_of(i * GRID_W, GRID_W), GRID_W), :] = o.astype(BF16)

    rows_per_iter = 8

    def body(j, carry):
        pending = [scores(rows_per_iter * j + u) for u in range(rows_per_iter)]
        for blk in pending:
            finish(*blk)
        return carry

    lax.fori_loop(0, rows // rows_per_iter, body, 0)


def _nbr_call(q, k, v, bias_tab):
    b, seq, _ = q.shape
    rows = seq // GRID_W
    pairs = D_COLS // LANES
    kern = functools.partial(_nbr_kernel, rows=rows)
    blk = lambda i, p: (i, 0, p)
    return pl.pallas_call(
        kern, grid=(b, pairs),
        in_specs=[pl.BlockSpec((1, seq, LANES), blk), pl.BlockSpec((1, seq, LANES), blk),
                  pl.BlockSpec((1, seq, LANES), blk),
                  pl.BlockSpec((2,) + bias_tab.shape[1:], lambda i, p: (p, 0, 0, 0))],
        out_specs=pl.BlockSpec((1, seq, LANES), blk),
        out_shape=jax.ShapeDtypeStruct((b, seq, D_COLS), BF16),
        compiler_params=pltpu.CompilerParams(dimension_semantics=("arbitrary",) * 2),
        name="mixer_d",
    )(q, k, v, bias_tab)


def _nbr_bias_table(rpb):
    cj = np.arange(GRID_W)
    c0 = np.clip(cj - NA_COLS // 2, 0, GRID_W - NA_COLS)
    col_ok = (cj[None, :] >= c0[:, None]) & (cj[None, :] < c0[:, None] + NA_COLS)
    dc = np.clip(cj[None, :] - cj[:, None], -(NA_COLS - 1), NA_COLS - 1) + (NA_COLS - 1)
    onehot = jnp.asarray(np.arange(2 * NA_COLS - 1)[:, None, None] == dc[None], dtype=F32)
    per_row = jnp.einsum('hrj,jqk->hrqk', rpb.astype(F32), onehot, precision=lax.Precision.HIGHEST)
    per_row = jnp.where(col_ok[None, None], per_row, NEG_INF)
    return jnp.concatenate([per_row[:, :-1], per_row[:, 1:]], axis=-1)


def _merge_mlp_kernel(x_ref, ya_ref, ob1_ref, ob2_ref, ob3_ref, lb1_ref, lb2_ref, lb3_ref, yc_ref, yd_ref,
                      ga_ref, gm_ref, wg_ref, wa_ref, wb_ref, wc_ref, wd_ref, wo_ref, wu_ref, wdn_ref, o_ref):
    x = x_ref[...]
    h = _rms_rows(x, ga_ref[...]).astype(BF16)
    l1, l2, l3 = lb1_ref[...], lb2_ref[...], lb3_ref[...]
    m = jnp.maximum(jnp.maximum(l1, l2), l3)
    w1, w2, w3 = jnp.exp(l1 - m), jnp.exp(l2 - m), jnp.exp(l3 - m)
    yb = (w1 * ob1_ref[...] + w2 * ob2_ref[...] + w3 * ob3_ref[...]) / (w1 + w2 + w3)
    branches = (ya_ref[...], yb.astype(BF16), yc_ref[...], yd_ref[...])
    weights = (wa_ref, wb_ref, wc_ref, wd_ref)
    merged = None
    for c in range(N_BRANCHES):
        z = _dot(h, wg_ref[:, c * D_MODEL:(c + 1) * D_MODEL])
        term = _dot(branches[c], weights[c][...]) / (1.0 + jnp.exp(-z))
        merged = term if merged is None else merged + term
    x1 = x + _dot(merged.astype(BF16), wo_ref[...])
    hm = _rms_rows(x1, gm_ref[...]).astype(BF16)
    acc = x1
    n_chunks = D_MLP // D_MODEL
    up = lambda c: _dot(hm, wu_ref[:, c * D_MODEL:(c + 1) * D_MODEL])
    u_next = up(0)
    for c in range(n_chunks):
        u = u_next
        if c + 1 < n_chunks:
            u_next = up(c + 1)
        u = jnp.square(jnp.maximum(u, 0.0)).astype(BF16)
        acc = acc + _dot(u, wdn_ref[c * D_MODEL:(c + 1) * D_MODEL, :])
    o_ref[...] = acc


def _merge_mlp_call(x2d, ya, obs, lbs, yc, yd, ga, gm, wg, wa, wb, wc, wd, wo, wu, wdn, tm):
    t_tokens = x2d.shape[0]
    row = lambda i: (i, 0)
    const = lambda i: (0, 0)
    single = pl.Buffered(1)
    acts = [x2d, ya, *obs, *lbs, yc, yd]
    consts = [ga, gm, wg, wa, wb, wc, wd, wo, wu, wdn]
    in_specs = ([pl.BlockSpec((tm, a.shape[1]), row) for a in acts]
                + [pl.BlockSpec(w.shape, const, pipeline_mode=single) for w in consts])
    return pl.pallas_call(
        _merge_mlp_kernel, grid=(t_tokens // tm,), in_specs=in_specs,
        out_specs=pl.BlockSpec((tm, D_MODEL), row),
        out_shape=jax.ShapeDtypeStruct((t_tokens, D_MODEL), F32),
        compiler_params=pltpu.CompilerParams(dimension_semantics=("arbitrary",), vmem_limit_bytes=VMEM_LIMIT),
        name="merge_mlp",
    )(*acts, *consts)


def _rope_lane_tables(seq, dim):
    inv = ROPE_THETA ** (-jnp.arange(0, dim, 2, dtype=F32) / dim)
    ang = jnp.arange(seq, dtype=F32)[:, None] * inv[None, :]
    cos, sin = jnp.cos(ang), jnp.sin(ang)
    reps = MXU_COLS // dim
    return (jnp.tile(jnp.concatenate([cos, cos], axis=-1), (1, reps)),
            jnp.tile(jnp.concatenate([-sin, sin], axis=-1), (1, reps)))


def _blockdiag_ones(seg):
    idx = np.arange(MXU_COLS) // seg
    return jnp.asarray(idx[:, None] == idx[None, :], dtype=BF16)


def _gain_row(l, a_g, b_g, c_g, d_g):
    ones = lambda n: jnp.ones((n,), F32)
    s64, s32 = HEAD_DIM ** -0.5, C_QK_DIM ** -0.5
    parts = [
        jnp.tile(a_g[l, 0] * (s64 * LOG2E), A_Q_HEADS), jnp.tile(a_g[l, 1], A_KV_HEADS), ones(A_KV_COLS),
        jnp.tile(b_g[l, 0] * s64, B_HEADS), jnp.tile(b_g[l, 1], B_HEADS), ones(B_COLS),
        jnp.tile(c_g[l, 0] * (s32 * LOG2E), 2 * C_HEADS), jnp.tile(c_g[l, 1], 2 * C_HEADS), ones(C_V_COLS),
        jnp.tile(d_g[l, 0] * s64, D_HEADS), jnp.tile(d_g[l, 1], D_HEADS), ones(D_COLS),
    ]
    return jnp.concatenate(parts).astype(F32)[None, :]


def kernel(x, attn_norm_g, w_in, a_qk_norm_g, a_sink, b_qk_norm_g, c_qk_norm_g, c_lambda, c_subln_g,
           d_qk_norm_g, d_rel_bias, w_branch_a, w_branch_b, w_branch_c, w_branch_d, w_out, mlp_norm_g,
           w_up, w_down):
    b, seq, _ = x.shape
    depth = w_in.shape[0]
    tokens = b * seq
    tm, tm_proj, tq_c = TILES["merge_rows"], TILES["proj_rows"], TILES["c_queries"]
    cos64, sin64 = _rope_lane_tables(seq, HEAD_DIM)
    cos32, sin32 = _rope_lane_tables(seq, C_QK_DIM)
    tabs = dict(cos64=cos64, sin64=sin64, cos32=cos32, sin32=sin32,
                e64=_blockdiag_ones(HEAD_DIM), e32=_blockdiag_ones(C_QK_DIM))
    xf = x.reshape(tokens, D_MODEL)
    for l in range(depth):
        w_l = w_in[l].astype(BF16)
        gains = _gain_row(l, a_qk_norm_g, b_qk_norm_g, c_qk_norm_g, d_qk_norm_g)
        w_aq = w_l[:, :A_Q_COLS].reshape(D_MODEL, A_Q_HEADS, HEAD_DIM)[:, A_HEAD_ORDER, :].reshape(D_MODEL, A_Q_COLS)
        wqkv = jnp.concatenate([w_aq, w_l[:, A_Q_COLS:QKV_COLS]], axis=1)
        wt = jnp.concatenate([w_l[:, _SECTION_COL[n]:_SECTION_COL[n] + dict((s[0], s[1]) for s in _SECTIONS)[n]].T
                              for n in _TRANSPOSED], axis=0)
        w_ba = (w_branch_a[l].astype(BF16).reshape(A_Q_HEADS, HEAD_DIM, D_MODEL)[A_HEAD_ORDER, :, :]
                .reshape(A_Q_COLS, D_MODEL))
        outs = _proj_call(xf, attn_norm_g[l][None, :], wqkv, wt, gains, tabs, seq, tm_proj)
        aq, ak, avt, bq, bk, bv, cq, ck, cvt, dq, dk, dv = outs
        tok3 = lambda t: t.reshape(b, seq, t.shape[-1])
        ya = _window_call(a_sink[l].astype(F32), tok3(aq), tok3(ak), avt, tq=TILES["a_queries"],
                          tiles=TILES["a_tiles_per_step"])
        obs, lbs = [], []
        for gi, (window, dil) in enumerate(B_PAIRS):
            o, lse = _dilated_call(tok3(bq), tok3(bk), tok3(bv), gi, window, dil, tq=TILES["b_queries"])
            obs.append(o.reshape(tokens, LANES))
            lbs.append(lse.reshape(tokens, LANES))
        lambda_init = 0.8 - 0.6 * math.exp(-0.3 * l)
        subg = jnp.broadcast_to(jnp.tile(c_subln_g[l].astype(F32), LANES // C_V_DIM)[:, None], (LANES, tq_c))
        logit_bound = (C_QK_DIM ** 0.5 * LOG2E) * jnp.max(jnp.abs(c_qk_norm_g[l, 0])) * jnp.max(jnp.abs(c_qk_norm_g[l, 1]))
        c_args = (tok3(cq), tok3(ck), cvt, c_lambda[l].astype(F32), subg)
        yc = lax.cond(
            logit_bound < C_LAGGED_LOGIT_LIMIT,
            lambda *a: _diff_call(*a, lambda_init, tq=tq_c, tk=TILES["c_keys"], lagged=True),
            lambda *a: _diff_call(*a, lambda_init, tq=tq_c, tk=TILES["c_keys_exact"], lagged=False),
            *c_args)
        yd = _nbr_call(tok3(dq), tok3(dk), tok3(dv), _nbr_bias_table(d_rel_bias[l]))
        xf = _merge_mlp_call(xf, ya.reshape(tokens, A_Q_COLS), obs, lbs, yc.reshape(tokens, C_V_COLS),
                             yd.reshape(tokens, D_COLS), attn_norm_g[l][None, :], mlp_norm_g[l][None, :],
                             w_l[:, QKV_COLS:], w_ba, w_branch_b[l].astype(BF16), w_branch_c[l].astype(BF16),
                             w_branch_d[l].astype(BF16), w_out[l].astype(BF16), w_up[l].astype(BF16),
                             w_down[l].astype(BF16), tm)
    return xf.reshape(b, seq, D_MODEL)
```

```python
import functools
import math

import jax
import jax.numpy as jnp
import numpy as np
from jax import lax
from jax.experimental import pallas as pl
from jax.experimental.pallas import tpu as pltpu

F32 = jnp.float32
BF16 = jnp.bfloat16

D_MODEL = 1024
HEAD_DIM = 64
GRID_W = 64
ROPE_THETA = 10000.0
NORM_EPS = 1e-6
NEG_INF = -1e30

A_Q_HEADS = 8
A_KV_HEADS = 2
A_RADIUS = 128
B_PAIRS = ((128, 1), (512, 4), (2048, 16))
B_HEADS_PER_GROUP = 2
B_HEADS = B_HEADS_PER_GROUP * len(B_PAIRS)
C_HEADS = 4
C_QK_DIM = 32
C_V_DIM = 2 * C_QK_DIM
D_HEADS = 4
NA_ROWS = 8
NA_COLS = 16
D_MLP = 4 * D_MODEL
N_BRANCHES = 4

A_Q_COLS = A_Q_HEADS * HEAD_DIM
A_KV_COLS = A_KV_HEADS * HEAD_DIM
B_COLS = B_HEADS * HEAD_DIM
C_QK_COLS = C_HEADS * 2 * C_QK_DIM
C_V_COLS = C_HEADS * C_V_DIM
D_COLS = D_HEADS * HEAD_DIM
GATE_COLS = N_BRANCHES * D_MODEL
QKV_COLS = (A_Q_COLS + 2 * A_KV_COLS + 3 * B_COLS + 2 * C_QK_COLS + C_V_COLS + 3 * D_COLS)

LANES = 128
MXU_COLS = 256
BF16_ROWS = 16
VMEM_LIMIT = 48 * 1024 * 1024
TILES = dict(proj_rows=512, merge_rows=256, a_queries=128, a_tiles_per_step=8, b_queries=128,
             c_queries=256, c_tiles_per_step=4, c_keys=256, c_tiles_per_step_exact=2, c_keys_exact=512)

_SECTIONS = (
    ("aq", A_Q_COLS, HEAD_DIM, True, BF16), ("ak", A_KV_COLS, HEAD_DIM, True, BF16), ("av", A_KV_COLS, None, False, BF16),
    ("bq", B_COLS, HEAD_DIM, True, F32), ("bk", B_COLS, HEAD_DIM, True, F32), ("bv", B_COLS, None, False, F32),
    ("cq", C_QK_COLS, C_QK_DIM, True, BF16), ("ck", C_QK_COLS, C_QK_DIM, True, BF16), ("cv", C_V_COLS, None, False, BF16),
    ("dq", D_COLS, HEAD_DIM, False, BF16), ("dk", D_COLS, HEAD_DIM, False, BF16), ("dv", D_COLS, None, False, BF16),
)
_SECTION_COL = {s[0]: sum(t[1] for t in _SECTIONS[:i]) for i, s in enumerate(_SECTIONS)}
_TRANSPOSED = ("av", "cv")
_TRANSPOSED_ROW = {n: sum(dict((s[0], s[1]) for s in _SECTIONS)[m] for m in _TRANSPOSED[:i])
                   for i, n in enumerate(_TRANSPOSED)}
_TRANSPOSED_ROWS = sum(s[1] for s in _SECTIONS if s[0] in _TRANSPOSED)
A_HEAD_ORDER = np.array([h for c in range(A_Q_HEADS // A_KV_HEADS) for h in (c, c + A_Q_HEADS // A_KV_HEADS)])
LOG2E = math.log2(math.e)
C_LAGGED_LOGIT_LIMIT = 16.0


def _nt_dot(a, b):
    return lax.dot_general(a, b, (((1,), (1,)), ((), ())), preferred_element_type=F32)


def _dot(a, b):
    return jnp.dot(a, b, preferred_element_type=F32)


def _rms_rows(x, g):
    ms = jnp.mean(x * x, axis=-1, keepdims=True)
    return x * lax.rsqrt(ms + NORM_EPS) * g


def _rotate_half(y, half):
    width = y.shape[1]
    lane = lax.broadcasted_iota(jnp.int32, y.shape, 1)
    first = (lane & (2 * half - 1)) < half
    return jnp.where(first, pltpu.roll(y, width - half, 1), pltpu.roll(y, half, 1))


def _proj_kernel(x_ref, g_ref, wqkv_ref, wt_ref, gains_ref, cos64_ref, sin64_ref, cos32_ref, sin32_ref,
                 e64_ref, e32_ref, *out_refs):
    x = x_ref[...]
    h = _rms_rows(x, g_ref[...]).astype(BF16)

    def project(idx):
        name, width = _SECTIONS[idx][:2]
        if name in _TRANSPOSED:
            r0 = _TRANSPOSED_ROW[name]
            return _nt_dot(wt_ref[r0:r0 + width, :], h)
        return _dot(h, wqkv_ref[:, _SECTION_COL[name]:_SECTION_COL[name] + width])

    t_next = project(0)
    for idx, (name, width, hd, rope, dtype) in enumerate(_SECTIONS):
        o_ref = out_refs[idx]
        col = _SECTION_COL[name]
        t = t_next
        if idx + 1 < len(_SECTIONS):
            t_next = project(idx + 1)
        if hd is None:
            o_ref[...] = t.astype(dtype)
        else:
            e_ref = e64_ref if hd == HEAD_DIM else e32_ref
            cos_ref, sin_ref = (cos64_ref, sin64_ref) if hd == HEAD_DIM else (cos32_ref, sin32_ref)
            for off in range(0, width, MXU_COLS):
                pw = min(MXU_COLS, width - off)
                tc = t[:, off:off + pw]
                ss = _dot((tc * tc).astype(BF16), e_ref[:pw, :pw])
                gain = gains_ref[:, col + off:col + off + pw]
                y = tc * lax.rsqrt(ss * (1.0 / hd) + NORM_EPS) * gain
                if rope:
                    y = y * cos_ref[:, :pw] + _rotate_half(y, hd // 2) * sin_ref[:, :pw]
                o_ref[:, off:off + pw] = y.astype(dtype)


def _proj_call(x2d, g, wqkv, wt, gains, tabs, seq, tm):
    t_tokens = x2d.shape[0]
    n_seq_tiles = seq // tm
    const = lambda i: (0, 0)
    row = lambda i: (i, 0)
    pos = lambda i: (i % n_seq_tiles, 0)
    single = pl.Buffered(1)
    in_specs = [
        pl.BlockSpec((tm, D_MODEL), row),
        pl.BlockSpec((1, D_MODEL), const),
        pl.BlockSpec((D_MODEL, QKV_COLS), const, pipeline_mode=single),
        pl.BlockSpec((_TRANSPOSED_ROWS, D_MODEL), const, pipeline_mode=single),
        pl.BlockSpec((1, QKV_COLS), const),
        pl.BlockSpec((tm, MXU_COLS), pos), pl.BlockSpec((tm, MXU_COLS), pos),
        pl.BlockSpec((tm, MXU_COLS), pos), pl.BlockSpec((tm, MXU_COLS), pos),
        pl.BlockSpec((MXU_COLS, MXU_COLS), const), pl.BlockSpec((MXU_COLS, MXU_COLS), const),
    ]
    out_shape, out_specs = [], []
    for name, w, _, _, dtype in _SECTIONS:
        if name in _TRANSPOSED:
            out_shape.append(jax.ShapeDtypeStruct((w, t_tokens), dtype))
            out_specs.append(pl.BlockSpec((w, tm), lambda i: (0, i)))
        else:
            out_shape.append(jax.ShapeDtypeStruct((t_tokens, w), dtype))
            out_specs.append(pl.BlockSpec((tm, w), row))
    return pl.pallas_call(
        _proj_kernel, grid=(t_tokens // tm,), in_specs=in_specs, out_specs=out_specs, out_shape=out_shape,
        compiler_params=pltpu.CompilerParams(dimension_semantics=("arbitrary",), vmem_limit_bytes=VMEM_LIMIT),
        name="proj",
    )(x2d, g, wqkv, wt, gains, tabs["cos64"], tabs["sin64"], tabs["cos32"], tabs["sin32"],
      tabs["e64"], tabs["e32"])


def _band_bias(n, tq, win, radius, start):
    qpos = n * tq + lax.broadcasted_iota(jnp.int32, (tq, win), 0)
    kpos = start + lax.broadcasted_iota(jnp.int32, (tq, win), 1)
    return jnp.where(jnp.abs(kpos - qpos) <= radius, 0.0, NEG_INF).astype(F32)


def _window_kernel(sink_ref, q_ref, k_ref, vt_ref, o_ref, *, tq, tiles, seq, radius):
    win = tq + 2 * radius
    left = lax.broadcasted_iota(jnp.int32, (tq, LANES), 1) < HEAD_DIM
    first = lax.broadcasted_iota(jnp.int32, (1, 2 * tq), 1) < tq
    rep = A_Q_HEADS // A_KV_HEADS
    ones = jnp.ones((BF16_ROWS, win), BF16)

    def window(sub):
        n = pl.program_id(1) * tiles + sub
        start = pl.multiple_of(jnp.clip(n * tq - radius, 0, seq - win), radius)
        kpos = start + lax.broadcasted_iota(jnp.int32, (win, tq), 0)
        qpos = n * tq + lax.broadcasted_iota(jnp.int32, (win, tq), 1)
        bias = jnp.where(jnp.abs(kpos - qpos) <= radius, 0.0, NEG_INF).astype(F32)
        return (k_ref[0, pl.ds(start, win), :], jnp.concatenate([vt_ref[:, pl.ds(start, win)], ones], axis=0),
                jnp.concatenate([bias, bias], axis=1))

    windows = [window(sub) for sub in range(tiles)]
    jobs = [(sub, c) for sub in range(tiles) for c in range(rep)]

    def scores(job):
        sub, c = job
        qc = q_ref[0, sub * tq:(sub + 1) * tq, c * LANES:(c + 1) * LANES]
        zero = jnp.zeros_like(qc)
        qm = jnp.concatenate([jnp.where(left, qc, zero), jnp.where(left, zero, qc)], axis=0)
        return _nt_dot(windows[sub][0], qm) + windows[sub][2]

    ahead = 4
    pending = [scores(job) for job in jobs[:ahead]]
    for j, (sub, c) in enumerate(jobs):
        st = pending.pop(0)
        if j + ahead < len(jobs):
            pending.append(scores(jobs[j + ahead]))
        sk = jnp.where(first, sink_ref[c], sink_ref[c + rep]) * LOG2E
        m = jnp.maximum(jnp.max(st, axis=0, keepdims=True), sk)
        pt = jnp.exp2(st - m).astype(BF16)
        acc = _dot(windows[sub][1], pt)
        den = acc[LANES:LANES + 1] + jnp.exp2(sk - m)
        ot = jnp.concatenate([acc[:HEAD_DIM, :tq] / den[:, :tq], acc[HEAD_DIM:LANES, tq:] / den[:, tq:]], axis=0)
        o_ref[0, sub * tq:(sub + 1) * tq, c * LANES:(c + 1) * LANES] = ot.T.astype(BF16)


def _window_call(sink, q, k, vt, tq, tiles):
    b, seq, _ = q.shape
    kern = functools.partial(_window_kernel, tq=tq, tiles=tiles, seq=seq, radius=A_RADIUS)
    return pl.pallas_call(
        kern, grid=(b, seq // (tq * tiles)),
        in_specs=[pl.BlockSpec(memory_space=pltpu.SMEM),
                  pl.BlockSpec((1, tq * tiles, A_Q_COLS), lambda i, n: (i, n, 0)),
                  pl.BlockSpec((1, seq, A_KV_COLS), lambda i, n: (i, 0, 0)),
                  pl.BlockSpec((A_KV_COLS, seq), lambda i, n: (0, i))],
        out_specs=pl.BlockSpec((1, tq * tiles, A_Q_COLS), lambda i, n: (i, n, 0)),
        out_shape=jax.ShapeDtypeStruct((b, seq, A_Q_COLS), BF16),
        compiler_params=pltpu.CompilerParams(dimension_semantics=("arbitrary", "arbitrary")),
        name="mixer_a",
    )(sink, q, k, vt)


def _dilated_kernel(q_ref, k_ref, v_ref, o_ref, lse_ref, *, dil, tq, seq, radius):
    sub_len = seq // dil
    n_tiles = sub_len // tq
    win = tq + 2 * radius

    def rows(first, count):
        if dil == 1:
            return pl.ds(pl.multiple_of(first, radius), count)
        return pl.ds(first, count, stride=dil)

    left = lax.broadcasted_iota(jnp.int32, (tq, LANES), 1) < HEAD_DIM
    ones = jnp.ones((win, LANES), BF16)

    blocks_per_iter = 4
    shared_window = win == sub_len and blocks_per_iter % n_tiles == 0

    def load_kv(r, start):
        kv_rows = rows(r + dil * start, win)
        return k_ref[0, kv_rows, :].astype(BF16), v_ref[0, kv_rows, :].astype(BF16)

    def scores(it, kv=None):
        r = it // n_tiles
        n = it - r * n_tiles
        start = jnp.clip(n * tq - radius, 0, sub_len - win)
        q_rows = rows(r + dil * (n * tq), tq)
        q = q_ref[0, q_rows, :].astype(BF16)
        k, v = load_kv(r, start) if kv is None else kv
        zero = jnp.zeros_like(q)
        q2 = jnp.concatenate([jnp.where(left, q, zero), jnp.where(left, zero, q)], axis=0)
        bias = _band_bias(n, tq, win, radius, start)
        return q_rows, v, _nt_dot(q2, k) + jnp.concatenate([bias, bias], axis=0)

    def finish(q_rows, v, s):
        m = jnp.max(s, axis=-1, keepdims=True)
        p = jnp.exp(s - m).astype(BF16)
        acc = _dot(p, jnp.concatenate([v, ones], axis=1))
        den = acc[:, LANES:]
        o2 = acc[:, :LANES] / den
        lse2 = jnp.log(den) + m
        o_ref[0, q_rows, :] = jnp.where(left, o2[:tq], o2[tq:])
        lse_ref[0, q_rows, :] = jnp.where(left, lse2[:tq], lse2[tq:])

    def body(j, carry):
        pending = []
        for u in range(blocks_per_iter):
            it = blocks_per_iter * j + u
            if shared_window and u % n_tiles == 0:
                kv = load_kv(it // n_tiles, 0)
            pending.append(scores(it, kv if shared_window else None))
        for blk in pending:
            finish(*blk)
        return carry

    lax.fori_loop(0, dil * n_tiles // blocks_per_iter, body, 0)


def _dilated_call(q, k, v, group, window, dil, tq):
    b, seq, _ = q.shape
    radius = window // (2 * dil)
    kern = functools.partial(_dilated_kernel, dil=dil, tq=tq, seq=seq, radius=radius)
    out_sds = jax.ShapeDtypeStruct((b, seq, LANES), F32)
    chunk = lambda i: (i, 0, group)
    whole = lambda i: (i, 0, 0)
    return pl.pallas_call(
        kern, grid=(b,),
        in_specs=[pl.BlockSpec((1, seq, LANES), chunk)] * 3,
        out_specs=[pl.BlockSpec((1, seq, LANES), whole)] * 2,
        out_shape=[out_sds, out_sds],
        compiler_params=pltpu.CompilerParams(dimension_semantics=("arbitrary",)),
        name=f"mixer_b_d{dil}",
    )(q, k, v)


def _diff_kernel(q_ref, k_ref, vt_ref, lam_ref, subg_ref, o_ref, *, tq, tiles, tk, seq, lambda_init, lagged):
    nseg = LANES // C_QK_DIM
    lane = lax.broadcasted_iota(jnp.int32, (tq, LANES), 1)

    def masked_rows(sub):
        q = q_ref[0, sub * tq:(sub + 1) * tq, :]
        zero = jnp.zeros_like(q)
        return jnp.concatenate([jnp.where((lane // C_QK_DIM) == j, q, zero) for j in range(nseg)], axis=0)

    q4 = [masked_rows(sub) for sub in range(tiles)]
    ones_rows = jnp.ones((BF16_ROWS, tk), BF16)
    m = [jnp.full((1, nseg * tq), NEG_INF, F32)] * tiles
    ref = list(m)
    acc = [jnp.zeros((LANES + BF16_ROWS, nseg * tq), F32)] * tiles
    n_blocks = seq // tk
    jobs = [(t, sub) for t in range(n_blocks) for sub in range(tiles)]
    scores = lambda job: _nt_dot(k_ref[0, job[0] * tk:(job[0] + 1) * tk, :], q4[job[1]])
    ahead = (1 if lagged else 2) * tiles
    pending = [scores(job) for job in jobs[:ahead]]
    for j, (t, sub) in enumerate(jobs):
        st = pending.pop(0)
        if j + ahead < len(jobs):
            pending.append(scores(jobs[j + ahead]))
        colmax = jnp.max(st, axis=0, keepdims=True)
        ref_new = m[sub] if (lagged and t > 0) else jnp.maximum(m[sub], colmax)
        alpha = jnp.exp2(ref[sub] - ref_new)
        pt = jnp.exp2(st - ref_new).astype(BF16)
        vext = jnp.concatenate([vt_ref[:, t * tk:(t + 1) * tk], ones_rows], axis=0)
        acc[sub] = acc[sub] * alpha + _dot(vext, pt)
        ref[sub] = ref_new
        m[sub] = jnp.maximum(m[sub], colmax)

    lp = lam_ref[...]
    lam = (jnp.exp(jnp.sum(lp[0:1] * lp[1:2], axis=-1, keepdims=True))
           - jnp.exp(jnp.sum(lp[2:3] * lp[3:4], axis=-1, keepdims=True)) + lambda_init)
    for sub in range(tiles):
        heads = []
        for hh in range(LANES // C_V_DIM):
            u = []
            for c in range(2):
                cols = slice((2 * hh + c) * tq, (2 * hh + c + 1) * tq)
                u.append(acc[sub][hh * C_V_DIM:(hh + 1) * C_V_DIM, cols] / acc[sub][LANES:LANES + 1, cols])
            o = u[0] - lam * u[1]
            ss = jnp.sum(o * o, axis=0, keepdims=True)
            heads.append(o * lax.rsqrt(ss * (1.0 / C_V_DIM) + NORM_EPS))
        ot = jnp.concatenate(heads, axis=0) * subg_ref[...] * (1.0 - lambda_init)
        o_ref[0, sub * tq:(sub + 1) * tq, :] = ot.T.astype(BF16)


def _diff_call(q, k, vt, lam_p, subg, lambda_init, tq, tiles, tk, lagged):
    b, seq, _ = q.shape
    pairs = C_QK_COLS // LANES
    kern = functools.partial(_diff_kernel, tq=tq, tiles=tiles, tk=tk, seq=seq, lambda_init=lambda_init,
                             lagged=lagged)
    return pl.pallas_call(
        kern, grid=(b, pairs, seq // (tq * tiles)),
        in_specs=[pl.BlockSpec((1, tq * tiles, LANES), lambda i, p, n: (i, n, p)),
                  pl.BlockSpec((1, seq, LANES), lambda i, p, n: (i, 0, p)),
                  pl.BlockSpec((LANES, seq), lambda i, p, n: (p, i)),
                  pl.BlockSpec((4, C_QK_DIM), lambda i, p, n: (0, 0)),
                  pl.BlockSpec((LANES, tq), lambda i, p, n: (0, 0))],
        out_specs=pl.BlockSpec((1, tq * tiles, LANES), lambda i, p, n: (i, n, p)),
        out_shape=jax.ShapeDtypeStruct((b, seq, C_V_COLS), BF16),
        compiler_params=pltpu.CompilerParams(dimension_semantics=("arbitrary",) * 3),
        name="mixer_c_lagged" if lagged else "mixer_c",
    )(q, k, vt, lam_p, subg)


def _nbr_kernel(q_ref, k_ref, v_ref, bias_ref, o_ref, *, rows):
    kr = NA_ROWS
    lane = lax.broadcasted_iota(jnp.int32, (GRID_W, LANES), 1)
    left = lane < HEAD_DIM

    ones = jnp.ones((kr * GRID_W, LANES), BF16)

    def scores(i):
        r0 = jnp.clip(i - kr // 2, 0, rows - kr)
        cls = jnp.minimum(i, kr // 2) + jnp.maximum(i - (rows - kr // 2), 0)
        q = q_ref[0, pl.ds(pl.multiple_of(i * GRID_W, GRID_W), GRID_W), :]
        koff = pl.multiple_of(r0 * GRID_W, GRID_W)
        k = k_ref[0, pl.ds(koff, kr * GRID_W), :]
        v = v_ref[0, pl.ds(koff, kr * GRID_W), :]
        zero = jnp.zeros_like(q)
        q2 = jnp.concatenate([jnp.where(left, q, zero), jnp.where(left, zero, q)], axis=0)
        bias = jnp.concatenate(
            [jnp.concatenate([bias_ref[hh, NA_ROWS - 1 - cls + 2 * u] for u in range(kr // 2)], axis=1)
             for hh in range(2)], axis=0)
        return i, v, _nt_dot(q2, k) + bias

    def finish(i, v, s):
        m = jnp.max(s, axis=-1, keepdims=True)
        p = jnp.exp(s - m).astype(BF16)
        acc = _dot(p, jnp.concatenate([v, ones], axis=1))
        pv = acc[:, :LANES] / acc[:, LANES:]
        o = jnp.where(left, pv[:GRID_W], pv[GRID_W:])
        o_ref[0, pl.ds(pl.multiple_of(i * GRID_W, GRID_W), GRID_W), :] = o.astype(BF16)

    rows_per_iter = 8

    def body(j, carry):
        pending = [scores(rows_per_iter * j + u) for u in range(rows_per_iter)]
        for blk in pending:
            finish(*blk)
        return carry

    lax.fori_loop(0, rows // rows_per_iter, body, 0)


def _nbr_call(q, k, v, bias_tab):
    b, seq, _ = q.shape
    rows = seq // GRID_W
    pairs = D_COLS // LANES
    kern = functools.partial(_nbr_kernel, rows=rows)
    blk = lambda i, p: (i, 0, p)
    return pl.pallas_call(
        kern, grid=(b, pairs),
        in_specs=[pl.BlockSpec((1, seq, LANES), blk), pl.BlockSpec((1, seq, LANES), blk),
                  pl.BlockSpec((1, seq, LANES), blk),
                  pl.BlockSpec((2,) + bias_tab.shape[1:], lambda i, p: (p, 0, 0, 0))],
        out_specs=pl.BlockSpec((1, seq, LANES), blk),
        out_shape=jax.ShapeDtypeStruct((b, seq, D_COLS), BF16),
        compiler_params=pltpu.CompilerParams(dimension_semantics=("arbitrary",) * 2),
        name="mixer_d",
    )(q, k, v, bias_tab)


def _nbr_bias_table(rpb):
    cj = np.arange(GRID_W)
    c0 = np.clip(cj - NA_COLS // 2, 0, GRID_W - NA_COLS)
    col_ok = (cj[None, :] >= c0[:, None]) & (cj[None, :] < c0[:, None] + NA_COLS)
    dc = np.clip(cj[None, :] - cj[:, None], -(NA_COLS - 1), NA_COLS - 1) + (NA_COLS - 1)
    onehot = jnp.asarray(np.arange(2 * NA_COLS - 1)[:, None, None] == dc[None], dtype=F32)
    per_row = jnp.einsum('hrj,jqk->hrqk', rpb.astype(F32), onehot, precision=lax.Precision.HIGHEST)
    per_row = jnp.where(col_ok[None, None], per_row, NEG_INF)
    return jnp.concatenate([per_row[:, :-1], per_row[:, 1:]], axis=-1)


def _merge_mlp_kernel(x_ref, ya_ref, ob1_ref, ob2_ref, ob3_ref, lb1_ref, lb2_ref, lb3_ref, yc_ref, yd_ref,
                      ga_ref, gm_ref, wg_ref, wa_ref, wb_ref, wc_ref, wd_ref, wo_ref, wu_ref, wdn_ref, o_ref):
    x = x_ref[...]
    h = _rms_rows(x, ga_ref[...]).astype(BF16)
    l1, l2, l3 = lb1_ref[...], lb2_ref[...], lb3_ref[...]
    m = jnp.maximum(jnp.maximum(l1, l2), l3)
    w1, w2, w3 = jnp.exp(l1 - m), jnp.exp(l2 - m), jnp.exp(l3 - m)
    yb = (w1 * ob1_ref[...] + w2 * ob2_ref[...] + w3 * ob3_ref[...]) / (w1 + w2 + w3)
    branches = (ya_ref[...], yb.astype(BF16), yc_ref[...], yd_ref[...])
    weights = (wa_ref, wb_ref, wc_ref, wd_ref)
    merged = None
    for c in range(N_BRANCHES):
        z = _dot(h, wg_ref[:, c * D_MODEL:(c + 1) * D_MODEL])
        term = _dot(branches[c], weights[c][...]) / (1.0 + jnp.exp(-z))
        merged = term if merged is None else merged + term
    x1 = x + _dot(merged.astype(BF16), wo_ref[...])
    hm = _rms_rows(x1, gm_ref[...]).astype(BF16)
    acc = x1
    n_chunks = D_MLP // D_MODEL
    up = lambda c: _dot(hm, wu_ref[:, c * D_MODEL:(c + 1) * D_MODEL])
    u_next = up(0)
    for c in range(n_chunks):
        u = u_next
        if c + 1 < n_chunks:
            u_next = up(c + 1)
        u = jnp.square(jnp.maximum(u, 0.0)).astype(BF16)
        acc = acc + _dot(u, wdn_ref[c * D_MODEL:(c + 1) * D_MODEL, :])
    o_ref[...] = acc


def _merge_mlp_call(x2d, ya, obs, lbs, yc, yd, ga, gm, wg, wa, wb, wc, wd, wo, wu, wdn, tm):
    t_tokens = x2d.shape[0]
    row = lambda i: (i, 0)
    const = lambda i: (0, 0)
    single = pl.Buffered(1)
    acts = [x2d, ya, *obs, *lbs, yc, yd]
    consts = [ga, gm, wg, wa, wb, wc, wd, wo, wu, wdn]
    in_specs = ([pl.BlockSpec((tm, a.shape[1]), row) for a in acts]
                + [pl.BlockSpec(w.shape, const, pipeline_mode=single) for w in consts])
    return pl.pallas_call(
        _merge_mlp_kernel, grid=(t_tokens // tm,), in_specs=in_specs,
        out_specs=pl.BlockSpec((tm, D_MODEL), row),
        out_shape=jax.ShapeDtypeStruct((t_tokens, D_MODEL), F32),
        compiler_params=pltpu.CompilerParams(dimension_semantics=("arbitrary",), vmem_limit_bytes=VMEM_LIMIT),
        name="merge_mlp",
    )(*acts, *consts)


def _rope_lane_tables(seq, dim):
    inv = ROPE_THETA ** (-jnp.arange(0, dim, 2, dtype=F32) / dim)
    ang = jnp.arange(seq, dtype=F32)[:, None] * inv[None, :]
    cos, sin = jnp.cos(ang), jnp.sin(ang)
    reps = MXU_COLS // dim
    return (jnp.tile(jnp.concatenate([cos, cos], axis=-1), (1, reps)),
            jnp.tile(jnp.concatenate([-sin, sin], axis=-1), (1, reps)))


def _blockdiag_ones(seg):
    idx = np.arange(MXU_COLS) // seg
    return jnp.asarray(idx[:, None] == idx[None, :], dtype=BF16)


def _gain_row(l, a_g, b_g, c_g, d_g):
    ones = lambda n: jnp.ones((n,), F32)
    s64, s32 = HEAD_DIM ** -0.5, C_QK_DIM ** -0.5
    parts = [
        jnp.tile(a_g[l, 0] * (s64 * LOG2E), A_Q_HEADS), jnp.tile(a_g[l, 1], A_KV_HEADS), ones(A_KV_COLS),
        jnp.tile(b_g[l, 0] * s64, B_HEADS), jnp.tile(b_g[l, 1], B_HEADS), ones(B_COLS),
        jnp.tile(c_g[l, 0] * (s32 * LOG2E), 2 * C_HEADS), jnp.tile(c_g[l, 1], 2 * C_HEADS), ones(C_V_COLS),
        jnp.tile(d_g[l, 0] * s64, D_HEADS), jnp.tile(d_g[l, 1], D_HEADS), ones(D_COLS),
    ]
    return jnp.concatenate(parts).astype(F32)[None, :]


def kernel(x, attn_norm_g, w_in, a_qk_norm_g, a_sink, b_qk_norm_g, c_qk_norm_g, c_lambda, c_subln_g,
           d_qk_norm_g, d_rel_bias, w_branch_a, w_branch_b, w_branch_c, w_branch_d, w_out, mlp_norm_g,
           w_up, w_down):
    b, seq, _ = x.shape
    depth = w_in.shape[0]
    tokens = b * seq
    tm, tm_proj, tq_c = TILES["merge_rows"], TILES["proj_rows"], TILES["c_queries"]
    cos64, sin64 = _rope_lane_tables(seq, HEAD_DIM)
    cos32, sin32 = _rope_lane_tables(seq, C_QK_DIM)
    tabs = dict(cos64=cos64, sin64=sin64, cos32=cos32, sin32=sin32,
                e64=_blockdiag_ones(HEAD_DIM), e32=_blockdiag_ones(C_QK_DIM))
    xf = x.reshape(tokens, D_MODEL)
    for l in range(depth):
        w_l = w_in[l].astype(BF16)
        gains = _gain_row(l, a_qk_norm_g, b_qk_norm_g, c_qk_norm_g, d_qk_norm_g)
        w_aq = w_l[:, :A_Q_COLS].reshape(D_MODEL, A_Q_HEADS, HEAD_DIM)[:, A_HEAD_ORDER, :].reshape(D_MODEL, A_Q_COLS)
        wqkv = jnp.concatenate([w_aq, w_l[:, A_Q_COLS:QKV_COLS]], axis=1)
        wt = jnp.concatenate([w_l[:, _SECTION_COL[n]:_SECTION_COL[n] + dict((s[0], s[1]) for s in _SECTIONS)[n]].T
                              for n in _TRANSPOSED], axis=0)
        w_ba = (w_branch_a[l].astype(BF16).reshape(A_Q_HEADS, HEAD_DIM, D_MODEL)[A_HEAD_ORDER, :, :]
                .reshape(A_Q_COLS, D_MODEL))
        outs = _proj_call(xf, attn_norm_g[l][None, :], wqkv, wt, gains, tabs, seq, tm_proj)
        aq, ak, avt, bq, bk, bv, cq, ck, cvt, dq, dk, dv = outs
        tok3 = lambda t: t.reshape(b, seq, t.shape[-1])
        ya = _window_call(a_sink[l].astype(F32), tok3(aq), tok3(ak), avt, tq=TILES["a_queries"],
                          tiles=TILES["a_tiles_per_step"])
        obs, lbs = [], []
        for gi, (window, dil) in enumerate(B_PAIRS):
            o, lse = _dilated_call(tok3(bq), tok3(bk), tok3(bv), gi, window, dil, tq=TILES["b_queries"])
            obs.append(o.reshape(tokens, LANES))
            lbs.append(lse.reshape(tokens, LANES))
        lambda_init = 0.8 - 0.6 * math.exp(-0.3 * l)
        subg = jnp.broadcast_to(jnp.tile(c_subln_g[l].astype(F32), LANES // C_V_DIM)[:, None], (LANES, tq_c))
        logit_bound = (C_QK_DIM ** 0.5 * LOG2E) * jnp.max(jnp.abs(c_qk_norm_g[l, 0])) * jnp.max(jnp.abs(c_qk_norm_g[l, 1]))
        c_args = (tok3(cq), tok3(ck), cvt, c_lambda[l].astype(F32), subg)
        yc = lax.cond(
            logit_bound < C_LAGGED_LOGIT_LIMIT,
            lambda *a: _diff_call(*a, lambda_init, tq=tq_c, tiles=TILES["c_tiles_per_step"], tk=TILES["c_keys"],
                                  lagged=True),
            lambda *a: _diff_call(*a, lambda_init, tq=tq_c, tiles=TILES["c_tiles_per_step_exact"],
                                  tk=TILES["c_keys_exact"], lagged=False),
            *c_args)
        yd = _nbr_call(tok3(dq), tok3(dk), tok3(dv), _nbr_bias_table(d_rel_bias[l]))
        xf = _merge_mlp_call(xf, ya.reshape(tokens, A_Q_COLS), obs, lbs, yc.reshape(tokens, C_V_COLS),
                             yd.reshape(tokens, D_COLS), attn_norm_g[l][None, :], mlp_norm_g[l][None, :],
                             w_l[:, QKV_COLS:], w_ba, w_branch_b[l].astype(BF16), w_branch_c[l].astype(BF16),
                             w_branch_d[l].astype(BF16), w_out[l].astype(BF16), w_up[l].astype(BF16),
                             w_down[l].astype(BF16), tm)
    return xf.reshape(b, seq, D_MODEL)
```

```python
import functools
import math

import jax
import jax.numpy as jnp
import numpy as np
from jax import lax
from jax.experimental import pallas as pl
from jax.experimental.pallas import tpu as pltpu

F32 = jnp.float32
BF16 = jnp.bfloat16

D_MODEL = 1024
HEAD_DIM = 64
GRID_W = 64
ROPE_THETA = 10000.0
NORM_EPS = 1e-6
NEG_INF = -1e30

A_Q_HEADS = 8
A_KV_HEADS = 2
A_RADIUS = 128
B_PAIRS = ((128, 1), (512, 4), (2048, 16))
B_HEADS_PER_GROUP = 2
B_HEADS = B_HEADS_PER_GROUP * len(B_PAIRS)
C_HEADS = 4
C_QK_DIM = 32
C_V_DIM = 2 * C_QK_DIM
D_HEADS = 4
NA_ROWS = 8
NA_COLS = 16
D_MLP = 4 * D_MODEL
N_BRANCHES = 4

A_Q_COLS = A_Q_HEADS * HEAD_DIM
A_KV_COLS = A_KV_HEADS * HEAD_DIM
B_COLS = B_HEADS * HEAD_DIM
C_QK_COLS = C_HEADS * 2 * C_QK_DIM
C_V_COLS = C_HEADS * C_V_DIM
D_COLS = D_HEADS * HEAD_DIM
GATE_COLS = N_BRANCHES * D_MODEL
QKV_COLS = (A_Q_COLS + 2 * A_KV_COLS + 3 * B_COLS + 2 * C_QK_COLS + C_V_COLS + 3 * D_COLS)

LANES = 128
MXU_COLS = 256
BF16_ROWS = 16
VMEM_LIMIT = 56 * 1024 * 1024
TILES = dict(proj_rows=512, merge_rows=256, merge_tiles_per_step=2, a_queries=128, a_tiles_per_step=8, b_queries=128,
             c_queries=256, c_tiles_per_step=4, c_keys=256, c_tiles_per_step_exact=2, c_keys_exact=512)

_SECTIONS = (
    ("aq", A_Q_COLS, HEAD_DIM, True, BF16), ("ak", A_KV_COLS, HEAD_DIM, True, BF16), ("av", A_KV_COLS, None, False, BF16),
    ("bq", B_COLS, HEAD_DIM, True, F32), ("bk", B_COLS, HEAD_DIM, True, F32), ("bv", B_COLS, None, False, F32),
    ("cq", C_QK_COLS, C_QK_DIM, True, BF16), ("ck", C_QK_COLS, C_QK_DIM, True, BF16), ("cv", C_V_COLS, None, False, BF16),
    ("dq", D_COLS, HEAD_DIM, False, BF16), ("dk", D_COLS, HEAD_DIM, False, BF16), ("dv", D_COLS, None, False, BF16),
)
_SECTION_COL = {s[0]: sum(t[1] for t in _SECTIONS[:i]) for i, s in enumerate(_SECTIONS)}
_TRANSPOSED = ("av", "cv")
_TRANSPOSED_ROW = {n: sum(dict((s[0], s[1]) for s in _SECTIONS)[m] for m in _TRANSPOSED[:i])
                   for i, n in enumerate(_TRANSPOSED)}
_TRANSPOSED_ROWS = sum(s[1] for s in _SECTIONS if s[0] in _TRANSPOSED)
A_HEAD_ORDER = np.array([h for c in range(A_Q_HEADS // A_KV_HEADS) for h in (c, c + A_Q_HEADS // A_KV_HEADS)])
LOG2E = math.log2(math.e)
C_LAGGED_LOGIT_LIMIT = 16.0


def _nt_dot(a, b):
    return lax.dot_general(a, b, (((1,), (1,)), ((), ())), preferred_element_type=F32)


def _dot(a, b):
    return jnp.dot(a, b, preferred_element_type=F32)


def _rms_rows(x, g):
    ms = jnp.mean(x * x, axis=-1, keepdims=True)
    return x * lax.rsqrt(ms + NORM_EPS) * g


def _rotate_half(y, half):
    width = y.shape[1]
    lane = lax.broadcasted_iota(jnp.int32, y.shape, 1)
    first = (lane & (2 * half - 1)) < half
    return jnp.where(first, pltpu.roll(y, width - half, 1), pltpu.roll(y, half, 1))


def _proj_kernel(x_ref, g_ref, wqkv_ref, wt_ref, gains_ref, cos64_ref, sin64_ref, cos32_ref, sin32_ref,
                 e64_ref, e32_ref, *out_refs):
    x = x_ref[...]
    h = _rms_rows(x, g_ref[...]).astype(BF16)

    def project(idx):
        name, width = _SECTIONS[idx][:2]
        if name in _TRANSPOSED:
            r0 = _TRANSPOSED_ROW[name]
            return _nt_dot(wt_ref[r0:r0 + width, :], h)
        return _dot(h, wqkv_ref[:, _SECTION_COL[name]:_SECTION_COL[name] + width])

    t_next = project(0)
    for idx, (name, width, hd, rope, dtype) in enumerate(_SECTIONS):
        o_ref = out_refs[idx]
        col = _SECTION_COL[name]
        t = t_next
        if idx + 1 < len(_SECTIONS):
            t_next = project(idx + 1)
        if hd is None:
            o_ref[...] = t.astype(dtype)
        else:
            e_ref = e64_ref if hd == HEAD_DIM else e32_ref
            cos_ref, sin_ref = (cos64_ref, sin64_ref) if hd == HEAD_DIM else (cos32_ref, sin32_ref)
            for off in range(0, width, MXU_COLS):
                pw = min(MXU_COLS, width - off)
                tc = t[:, off:off + pw]
                ss = _dot((tc * tc).astype(BF16), e_ref[:pw, :pw])
                gain = gains_ref[:, col + off:col + off + pw]
                y = tc * lax.rsqrt(ss * (1.0 / hd) + NORM_EPS) * gain
                if rope:
                    y = y * cos_ref[:, :pw] + _rotate_half(y, hd // 2) * sin_ref[:, :pw]
                o_ref[:, off:off + pw] = y.astype(dtype)


def _proj_call(x2d, g, wqkv, wt, gains, tabs, seq, tm):
    t_tokens = x2d.shape[0]
    n_seq_tiles = seq // tm
    const = lambda i: (0, 0)
    row = lambda i: (i, 0)
    pos = lambda i: (i % n_seq_tiles, 0)
    single = pl.Buffered(1)
    in_specs = [
        pl.BlockSpec((tm, D_MODEL), row),
        pl.BlockSpec((1, D_MODEL), const),
        pl.BlockSpec((D_MODEL, QKV_COLS), const, pipeline_mode=single),
        pl.BlockSpec((_TRANSPOSED_ROWS, D_MODEL), const, pipeline_mode=single),
        pl.BlockSpec((1, QKV_COLS), const),
        pl.BlockSpec((tm, MXU_COLS), pos), pl.BlockSpec((tm, MXU_COLS), pos),
        pl.BlockSpec((tm, MXU_COLS), pos), pl.BlockSpec((tm, MXU_COLS), pos),
        pl.BlockSpec((MXU_COLS, MXU_COLS), const), pl.BlockSpec((MXU_COLS, MXU_COLS), const),
    ]
    out_shape, out_specs = [], []
    for name, w, _, _, dtype in _SECTIONS:
        if name in _TRANSPOSED:
            out_shape.append(jax.ShapeDtypeStruct((w, t_tokens), dtype))
            out_specs.append(pl.BlockSpec((w, tm), lambda i: (0, i)))
        else:
            out_shape.append(jax.ShapeDtypeStruct((t_tokens, w), dtype))
            out_specs.append(pl.BlockSpec((tm, w), row))
    return pl.pallas_call(
        _proj_kernel, grid=(t_tokens // tm,), in_specs=in_specs, out_specs=out_specs, out_shape=out_shape,
        compiler_params=pltpu.CompilerParams(dimension_semantics=("arbitrary",), vmem_limit_bytes=VMEM_LIMIT),
        name="proj",
    )(x2d, g, wqkv, wt, gains, tabs["cos64"], tabs["sin64"], tabs["cos32"], tabs["sin32"],
      tabs["e64"], tabs["e32"])


def _band_bias(n, tq, win, radius, start):
    qpos = n * tq + lax.broadcasted_iota(jnp.int32, (tq, win), 0)
    kpos = start + lax.broadcasted_iota(jnp.int32, (tq, win), 1)
    return jnp.where(jnp.abs(kpos - qpos) <= radius, 0.0, NEG_INF).astype(F32)


def _window_kernel(sink_ref, q_ref, k_ref, vt_ref, o_ref, *, tq, tiles, seq, radius):
    win = tq + 2 * radius
    left = lax.broadcasted_iota(jnp.int32, (tq, LANES), 1) < HEAD_DIM
    first = lax.broadcasted_iota(jnp.int32, (1, 2 * tq), 1) < tq
    rep = A_Q_HEADS // A_KV_HEADS
    ones = jnp.ones((BF16_ROWS, win), BF16)

    def window(sub):
        n = pl.program_id(1) * tiles + sub
        start = pl.multiple_of(jnp.clip(n * tq - radius, 0, seq - win), radius)
        kpos = start + lax.broadcasted_iota(jnp.int32, (win, tq), 0)
        qpos = n * tq + lax.broadcasted_iota(jnp.int32, (win, tq), 1)
        bias = jnp.where(jnp.abs(kpos - qpos) <= radius, 0.0, NEG_INF).astype(F32)
        return (k_ref[0, pl.ds(start, win), :], jnp.concatenate([vt_ref[:, pl.ds(start, win)], ones], axis=0),
                jnp.concatenate([bias, bias], axis=1))

    windows = [window(sub) for sub in range(tiles)]
    jobs = [(sub, c) for sub in range(tiles) for c in range(rep)]

    def scores(job):
        sub, c = job
        qc = q_ref[0, sub * tq:(sub + 1) * tq, c * LANES:(c + 1) * LANES]
        zero = jnp.zeros_like(qc)
        qm = jnp.concatenate([jnp.where(left, qc, zero), jnp.where(left, zero, qc)], axis=0)
        return _nt_dot(windows[sub][0], qm) + windows[sub][2]

    ahead = 4
    pending = [scores(job) for job in jobs[:ahead]]
    for j, (sub, c) in enumerate(jobs):
        st = pending.pop(0)
        if j + ahead < len(jobs):
            pending.append(scores(jobs[j + ahead]))
        sk = jnp.where(first, sink_ref[c], sink_ref[c + rep]) * LOG2E
        m = jnp.maximum(jnp.max(st, axis=0, keepdims=True), sk)
        pt = jnp.exp2(st - m).astype(BF16)
        acc = _dot(windows[sub][1], pt)
        den = acc[LANES:LANES + 1] + jnp.exp2(sk - m)
        ot = jnp.concatenate([acc[:HEAD_DIM, :tq] / den[:, :tq], acc[HEAD_DIM:LANES, tq:] / den[:, tq:]], axis=0)
        o_ref[0, sub * tq:(sub + 1) * tq, c * LANES:(c + 1) * LANES] = ot.T.astype(BF16)


def _window_call(sink, q, k, vt, tq, tiles):
    b, seq, _ = q.shape
    kern = functools.partial(_window_kernel, tq=tq, tiles=tiles, seq=seq, radius=A_RADIUS)
    return pl.pallas_call(
        kern, grid=(b, seq // (tq * tiles)),
        in_specs=[pl.BlockSpec(memory_space=pltpu.SMEM),
                  pl.BlockSpec((1, tq * tiles, A_Q_COLS), lambda i, n: (i, n, 0)),
                  pl.BlockSpec((1, seq, A_KV_COLS), lambda i, n: (i, 0, 0)),
                  pl.BlockSpec((A_KV_COLS, seq), lambda i, n: (0, i))],
        out_specs=pl.BlockSpec((1, tq * tiles, A_Q_COLS), lambda i, n: (i, n, 0)),
        out_shape=jax.ShapeDtypeStruct((b, seq, A_Q_COLS), BF16),
        compiler_params=pltpu.CompilerParams(dimension_semantics=("arbitrary", "arbitrary")),
        name="mixer_a",
    )(sink, q, k, vt)


def _dilated_kernel(q_ref, k_ref, v_ref, o_ref, lse_ref, *, dil, tq, seq, radius):
    sub_len = seq // dil
    n_tiles = sub_len // tq
    win = tq + 2 * radius

    def rows(first, count):
        if dil == 1:
            return pl.ds(pl.multiple_of(first, radius), count)
        return pl.ds(first, count, stride=dil)

    left = lax.broadcasted_iota(jnp.int32, (tq, LANES), 1) < HEAD_DIM
    ones = jnp.ones((win, LANES), BF16)

    blocks_per_iter = 4
    shared_window = win == sub_len and blocks_per_iter % n_tiles == 0

    def load_kv(r, start):
        kv_rows = rows(r + dil * start, win)
        return k_ref[0, kv_rows, :].astype(BF16), v_ref[0, kv_rows, :].astype(BF16)

    def scores(it, kv=None):
        r = it // n_tiles
        n = it - r * n_tiles
        start = jnp.clip(n * tq - radius, 0, sub_len - win)
        q_rows = rows(r + dil * (n * tq), tq)
        q = q_ref[0, q_rows, :].astype(BF16)
        k, v = load_kv(r, start) if kv is None else kv
        zero = jnp.zeros_like(q)
        q2 = jnp.concatenate([jnp.where(left, q, zero), jnp.where(left, zero, q)], axis=0)
        bias = _band_bias(n, tq, win, radius, start)
        return q_rows, v, _nt_dot(q2, k) + jnp.concatenate([bias, bias], axis=0)

    def finish(q_rows, v, s):
        m = jnp.max(s, axis=-1, keepdims=True)
        p = jnp.exp(s - m).astype(BF16)
        acc = _dot(p, jnp.concatenate([v, ones], axis=1))
        den = acc[:, LANES:]
        o2 = acc[:, :LANES] / den
        lse2 = jnp.log(den) + m
        o_ref[0, q_rows, :] = jnp.where(left, o2[:tq], o2[tq:])
        lse_ref[0, q_rows, :] = jnp.where(left, lse2[:tq], lse2[tq:])

    def body(j, carry):
        pending = []
        for u in range(blocks_per_iter):
            it = blocks_per_iter * j + u
            if shared_window and u % n_tiles == 0:
                kv = load_kv(it // n_tiles, 0)
            pending.append(scores(it, kv if shared_window else None))
        for blk in pending:
            finish(*blk)
        return carry

    lax.fori_loop(0, dil * n_tiles // blocks_per_iter, body, 0)


def _dilated_call(q, k, v, group, window, dil, tq):
    b, seq, _ = q.shape
    radius = window // (2 * dil)
    kern = functools.partial(_dilated_kernel, dil=dil, tq=tq, seq=seq, radius=radius)
    out_sds = jax.ShapeDtypeStruct((b, seq, LANES), F32)
    chunk = lambda i: (i, 0, group)
    whole = lambda i: (i, 0, 0)
    return pl.pallas_call(
        kern, grid=(b,),
        in_specs=[pl.BlockSpec((1, seq, LANES), chunk)] * 3,
        out_specs=[pl.BlockSpec((1, seq, LANES), whole)] * 2,
        out_shape=[out_sds, out_sds],
        compiler_params=pltpu.CompilerParams(dimension_semantics=("arbitrary",)),
        name=f"mixer_b_d{dil}",
    )(q, k, v)


def _diff_kernel(q_ref, k_ref, vt_ref, lam_ref, subg_ref, o_ref, *, tq, tiles, tk, seq, lambda_init, lagged):
    nseg = LANES // C_QK_DIM
    lane = lax.broadcasted_iota(jnp.int32, (tq, LANES), 1)

    def masked_rows(sub):
        q = q_ref[0, sub * tq:(sub + 1) * tq, :]
        zero = jnp.zeros_like(q)
        return jnp.concatenate([jnp.where((lane // C_QK_DIM) == j, q, zero) for j in range(nseg)], axis=0)

    q4 = [masked_rows(sub) for sub in range(tiles)]
    ones_rows = jnp.ones((BF16_ROWS, tk), BF16)
    m = [jnp.full((1, nseg * tq), NEG_INF, F32)] * tiles
    ref = list(m)
    acc = [jnp.zeros((LANES + BF16_ROWS, nseg * tq), F32)] * tiles
    n_blocks = seq // tk
    jobs = [(t, sub) for t in range(n_blocks) for sub in range(tiles)]
    scores = lambda job: _nt_dot(k_ref[0, job[0] * tk:(job[0] + 1) * tk, :], q4[job[1]])
    ahead = (1 if lagged else 2) * tiles
    pending = [scores(job) for job in jobs[:ahead]]
    for j, (t, sub) in enumerate(jobs):
        st = pending.pop(0)
        if j + ahead < len(jobs):
            pending.append(scores(jobs[j + ahead]))
        colmax = jnp.max(st, axis=0, keepdims=True)
        ref_new = m[sub] if (lagged and t > 0) else jnp.maximum(m[sub], colmax)
        alpha = jnp.exp2(ref[sub] - ref_new)
        pt = jnp.exp2(st - ref_new).astype(BF16)
        vext = jnp.concatenate([vt_ref[:, t * tk:(t + 1) * tk], ones_rows], axis=0)
        acc[sub] = acc[sub] * alpha + _dot(vext, pt)
        ref[sub] = ref_new
        m[sub] = jnp.maximum(m[sub], colmax)

    lp = lam_ref[...]
    lam = (jnp.exp(jnp.sum(lp[0:1] * lp[1:2], axis=-1, keepdims=True))
           - jnp.exp(jnp.sum(lp[2:3] * lp[3:4], axis=-1, keepdims=True)) + lambda_init)
    for sub in range(tiles):
        heads = []
        for hh in range(LANES // C_V_DIM):
            u = []
            for c in range(2):
                cols = slice((2 * hh + c) * tq, (2 * hh + c + 1) * tq)
                u.append(acc[sub][hh * C_V_DIM:(hh + 1) * C_V_DIM, cols] / acc[sub][LANES:LANES + 1, cols])
            o = u[0] - lam * u[1]
            ss = jnp.sum(o * o, axis=0, keepdims=True)
            heads.append(o * lax.rsqrt(ss * (1.0 / C_V_DIM) + NORM_EPS))
        ot = jnp.concatenate(heads, axis=0) * subg_ref[...] * (1.0 - lambda_init)
        o_ref[0, sub * tq:(sub + 1) * tq, :] = ot.T.astype(BF16)


def _diff_call(q, k, vt, lam_p, subg, lambda_init, tq, tiles, tk, lagged):
    b, seq, _ = q.shape
    pairs = C_QK_COLS // LANES
    kern = functools.partial(_diff_kernel, tq=tq, tiles=tiles, tk=tk, seq=seq, lambda_init=lambda_init,
                             lagged=lagged)
    return pl.pallas_call(
        kern, grid=(b, pairs, seq // (tq * tiles)),
        in_specs=[pl.BlockSpec((1, tq * tiles, LANES), lambda i, p, n: (i, n, p)),
                  pl.BlockSpec((1, seq, LANES), lambda i, p, n: (i, 0, p)),
                  pl.BlockSpec((LANES, seq), lambda i, p, n: (p, i)),
                  pl.BlockSpec((4, C_QK_DIM), lambda i, p, n: (0, 0)),
                  pl.BlockSpec((LANES, tq), lambda i, p, n: (0, 0))],
        out_specs=pl.BlockSpec((1, tq * tiles, LANES), lambda i, p, n: (i, n, p)),
        out_shape=jax.ShapeDtypeStruct((b, seq, C_V_COLS), BF16),
        compiler_params=pltpu.CompilerParams(dimension_semantics=("arbitrary",) * 3),
        name="mixer_c_lagged" if lagged else "mixer_c",
    )(q, k, vt, lam_p, subg)


def _nbr_kernel(q_ref, k_ref, v_ref, bias_ref, o_ref, *, rows):
    kr = NA_ROWS
    lane = lax.broadcasted_iota(jnp.int32, (GRID_W, LANES), 1)
    left = lane < HEAD_DIM

    ones = jnp.ones((kr * GRID_W, LANES), BF16)

    def scores(i):
        r0 = jnp.clip(i - kr // 2, 0, rows - kr)
        cls = jnp.minimum(i, kr // 2) + jnp.maximum(i - (rows - kr // 2), 0)
        q = q_ref[0, pl.ds(pl.multiple_of(i * GRID_W, GRID_W), GRID_W), :]
        koff = pl.multiple_of(r0 * GRID_W, GRID_W)
        k = k_ref[0, pl.ds(koff, kr * GRID_W), :]
        v = v_ref[0, pl.ds(koff, kr * GRID_W), :]
        zero = jnp.zeros_like(q)
        q2 = jnp.concatenate([jnp.where(left, q, zero), jnp.where(left, zero, q)], axis=0)
        bias = jnp.concatenate(
            [jnp.concatenate([bias_ref[hh, NA_ROWS - 1 - cls + 2 * u] for u in range(kr // 2)], axis=1)
             for hh in range(2)], axis=0)
        return i, v, _nt_dot(q2, k) + bias

    def finish(i, v, s):
        m = jnp.max(s, axis=-1, keepdims=True)
        p = jnp.exp(s - m).astype(BF16)
        acc = _dot(p, jnp.concatenate([v, ones], axis=1))
        pv = acc[:, :LANES] / acc[:, LANES:]
        o = jnp.where(left, pv[:GRID_W], pv[GRID_W:])
        o_ref[0, pl.ds(pl.multiple_of(i * GRID_W, GRID_W), GRID_W), :] = o.astype(BF16)

    rows_per_iter = 8

    def body(j, carry):
        pending = [scores(rows_per_iter * j + u) for u in range(rows_per_iter)]
        for blk in pending:
            finish(*blk)
        return carry

    lax.fori_loop(0, rows // rows_per_iter, body, 0)


def _nbr_call(q, k, v, bias_tab):
    b, seq, _ = q.shape
    rows = seq // GRID_W
    pairs = D_COLS // LANES
    kern = functools.partial(_nbr_kernel, rows=rows)
    blk = lambda i, p: (i, 0, p)
    return pl.pallas_call(
        kern, grid=(b, pairs),
        in_specs=[pl.BlockSpec((1, seq, LANES), blk), pl.BlockSpec((1, seq, LANES), blk),
                  pl.BlockSpec((1, seq, LANES), blk),
                  pl.BlockSpec((2,) + bias_tab.shape[1:], lambda i, p: (p, 0, 0, 0))],
        out_specs=pl.BlockSpec((1, seq, LANES), blk),
        out_shape=jax.ShapeDtypeStruct((b, seq, D_COLS), BF16),
        compiler_params=pltpu.CompilerParams(dimension_semantics=("arbitrary",) * 2),
        name="mixer_d",
    )(q, k, v, bias_tab)


def _nbr_bias_table(rpb):
    cj = np.arange(GRID_W)
    c0 = np.clip(cj - NA_COLS // 2, 0, GRID_W - NA_COLS)
    col_ok = (cj[None, :] >= c0[:, None]) & (cj[None, :] < c0[:, None] + NA_COLS)
    dc = np.clip(cj[None, :] - cj[:, None], -(NA_COLS - 1), NA_COLS - 1) + (NA_COLS - 1)
    onehot = jnp.asarray(np.arange(2 * NA_COLS - 1)[:, None, None] == dc[None], dtype=F32)
    per_row = jnp.einsum('hrj,jqk->hrqk', rpb.astype(F32), onehot, precision=lax.Precision.HIGHEST)
    per_row = jnp.where(col_ok[None, None], per_row, NEG_INF)
    return jnp.concatenate([per_row[:, :-1], per_row[:, 1:]], axis=-1)


def _merge_mlp_kernel(x_ref, ya_ref, ob1_ref, ob2_ref, ob3_ref, lb1_ref, lb2_ref, lb3_ref, yc_ref, yd_ref,
                      ga_ref, gm_ref, wg_ref, wa_ref, wb_ref, wc_ref, wd_ref, wo_ref, wu_ref, wdn_ref, o_ref,
                      *, rows, tiles):
    subs = [slice(s * rows, (s + 1) * rows) for s in range(tiles)]
    weights = (wa_ref, wb_ref, wc_ref, wd_ref)
    xs, merged = [], []
    for sl in subs:
        x = x_ref[sl, :]
        h = _rms_rows(x, ga_ref[...]).astype(BF16)
        l1, l2, l3 = lb1_ref[sl, :], lb2_ref[sl, :], lb3_ref[sl, :]
        m = jnp.maximum(jnp.maximum(l1, l2), l3)
        w1, w2, w3 = jnp.exp(l1 - m), jnp.exp(l2 - m), jnp.exp(l3 - m)
        yb = (w1 * ob1_ref[sl, :] + w2 * ob2_ref[sl, :] + w3 * ob3_ref[sl, :]) / (w1 + w2 + w3)
        branches = (ya_ref[sl, :], yb.astype(BF16), yc_ref[sl, :], yd_ref[sl, :])
        total = None
        for c in range(N_BRANCHES):
            z = _dot(h, wg_ref[:, c * D_MODEL:(c + 1) * D_MODEL])
            term = _dot(branches[c], weights[c][...]) / (1.0 + jnp.exp(-z))
            total = term if total is None else total + term
        xs.append(x)
        merged.append(total)
    x1 = [x + _dot(t.astype(BF16), wo_ref[...]) for x, t in zip(xs, merged)]
    hm = [_rms_rows(v, gm_ref[...]).astype(BF16) for v in x1]
    acc = list(x1)
    n_chunks = D_MLP // D_MODEL
    jobs = [(c, s) for c in range(n_chunks) for s in range(tiles)]
    up = lambda job: _dot(hm[job[1]], wu_ref[:, job[0] * D_MODEL:(job[0] + 1) * D_MODEL])
    pending = [up(job) for job in jobs[:tiles]]
    for j, (c, s) in enumerate(jobs):
        u = pending.pop(0)
        if j + tiles < len(jobs):
            pending.append(up(jobs[j + tiles]))
        u = jnp.square(jnp.maximum(u, 0.0)).astype(BF16)
        acc[s] = acc[s] + _dot(u, wdn_ref[c * D_MODEL:(c + 1) * D_MODEL, :])
    for s, sl in enumerate(subs):
        o_ref[sl, :] = acc[s]


def _merge_mlp_call(x2d, ya, obs, lbs, yc, yd, ga, gm, wg, wa, wb, wc, wd, wo, wu, wdn, tm, tiles):
    t_tokens = x2d.shape[0]
    row = lambda i: (i, 0)
    const = lambda i: (0, 0)
    single = pl.Buffered(1)
    acts = [x2d, ya, *obs, *lbs, yc, yd]
    consts = [ga, gm, wg, wa, wb, wc, wd, wo, wu, wdn]
    in_specs = ([pl.BlockSpec((tm * tiles, a.shape[1]), row) for a in acts]
                + [pl.BlockSpec(w.shape, const, pipeline_mode=single) for w in consts])
    return pl.pallas_call(
        functools.partial(_merge_mlp_kernel, rows=tm, tiles=tiles), grid=(t_tokens // (tm * tiles),),
        in_specs=in_specs, out_specs=pl.BlockSpec((tm * tiles, D_MODEL), row),
        out_shape=jax.ShapeDtypeStruct((t_tokens, D_MODEL), F32),
        compiler_params=pltpu.CompilerParams(dimension_semantics=("arbitrary",), vmem_limit_bytes=VMEM_LIMIT),
        name="merge_mlp",
    )(*acts, *consts)


def _rope_lane_tables(seq, dim):
    inv = ROPE_THETA ** (-jnp.arange(0, dim, 2, dtype=F32) / dim)
    ang = jnp.arange(seq, dtype=F32)[:, None] * inv[None, :]
    cos, sin = jnp.cos(ang), jnp.sin(ang)
    reps = MXU_COLS // dim
    return (jnp.tile(jnp.concatenate([cos, cos], axis=-1), (1, reps)),
            jnp.tile(jnp.concatenate([-sin, sin], axis=-1), (1, reps)))


def _blockdiag_ones(seg):
    idx = np.arange(MXU_COLS) // seg
    return jnp.asarray(idx[:, None] == idx[None, :], dtype=BF16)


def _gain_row(l, a_g, b_g, c_g, d_g):
    ones = lambda n: jnp.ones((n,), F32)
    s64, s32 = HEAD_DIM ** -0.5, C_QK_DIM ** -0.5
    parts = [
        jnp.tile(a_g[l, 0] * (s64 * LOG2E), A_Q_HEADS), jnp.tile(a_g[l, 1], A_KV_HEADS), ones(A_KV_COLS),
        jnp.tile(b_g[l, 0] * s64, B_HEADS), jnp.tile(b_g[l, 1], B_HEADS), ones(B_COLS),
        jnp.tile(c_g[l, 0] * (s32 * LOG2E), 2 * C_HEADS), jnp.tile(c_g[l, 1], 2 * C_HEADS), ones(C_V_COLS),
        jnp.tile(d_g[l, 0] * s64, D_HEADS), jnp.tile(d_g[l, 1], D_HEADS), ones(D_COLS),
    ]
    return jnp.concatenate(parts).astype(F32)[None, :]


def kernel(x, attn_norm_g, w_in, a_qk_norm_g, a_sink, b_qk_norm_g, c_qk_norm_g, c_lambda, c_subln_g,
           d_qk_norm_g, d_rel_bias, w_branch_a, w_branch_b, w_branch_c, w_branch_d, w_out, mlp_norm_g,
           w_up, w_down):
    b, seq, _ = x.shape
    depth = w_in.shape[0]
    tokens = b * seq
    tm, tm_proj, tq_c = TILES["merge_rows"], TILES["proj_rows"], TILES["c_queries"]
    cos64, sin64 = _rope_lane_tables(seq, HEAD_DIM)
    cos32, sin32 = _rope_lane_tables(seq, C_QK_DIM)
    tabs = dict(cos64=cos64, sin64=sin64, cos32=cos32, sin32=sin32,
                e64=_blockdiag_ones(HEAD_DIM), e32=_blockdiag_ones(C_QK_DIM))
    xf = x.reshape(tokens, D_MODEL)
    for l in range(depth):
        w_l = w_in[l].astype(BF16)
        gains = _gain_row(l, a_qk_norm_g, b_qk_norm_g, c_qk_norm_g, d_qk_norm_g)
        w_aq = w_l[:, :A_Q_COLS].reshape(D_MODEL, A_Q_HEADS, HEAD_DIM)[:, A_HEAD_ORDER, :].reshape(D_MODEL, A_Q_COLS)
        wqkv = jnp.concatenate([w_aq, w_l[:, A_Q_COLS:QKV_COLS]], axis=1)
        wt = jnp.concatenate([w_l[:, _SECTION_COL[n]:_SECTION_COL[n] + dict((s[0], s[1]) for s in _SECTIONS)[n]].T
                              for n in _TRANSPOSED], axis=0)
        w_ba = (w_branch_a[l].astype(BF16).reshape(A_Q_HEADS, HEAD_DIM, D_MODEL)[A_HEAD_ORDER, :, :]
                .reshape(A_Q_COLS, D_MODEL))
        outs = _proj_call(xf, attn_norm_g[l][None, :], wqkv, wt, gains, tabs, seq, tm_proj)
        aq, ak, avt, bq, bk, bv, cq, ck, cvt, dq, dk, dv = outs
        tok3 = lambda t: t.reshape(b, seq, t.shape[-1])
        ya = _window_call(a_sink[l].astype(F32), tok3(aq), tok3(ak), avt, tq=TILES["a_queries"],
                          tiles=TILES["a_tiles_per_step"])
        obs, lbs = [], []
        for gi, (window, dil) in enumerate(B_PAIRS):
            o, lse = _dilated_call(tok3(bq), tok3(bk), tok3(bv), gi, window, dil, tq=TILES["b_queries"])
            obs.append(o.reshape(tokens, LANES))
            lbs.append(lse.reshape(tokens, LANES))
        lambda_init = 0.8 - 0.6 * math.exp(-0.3 * l)
        subg = jnp.broadcast_to(jnp.tile(c_subln_g[l].astype(F32), LANES // C_V_DIM)[:, None], (LANES, tq_c))
        logit_bound = (C_QK_DIM ** 0.5 * LOG2E) * jnp.max(jnp.abs(c_qk_norm_g[l, 0])) * jnp.max(jnp.abs(c_qk_norm_g[l, 1]))
        c_args = (tok3(cq), tok3(ck), cvt, c_lambda[l].astype(F32), subg)
        yc = lax.cond(
            logit_bound < C_LAGGED_LOGIT_LIMIT,
            lambda *a: _diff_call(*a, lambda_init, tq=tq_c, tiles=TILES["c_tiles_per_step"], tk=TILES["c_keys"],
                                  lagged=True),
            lambda *a: _diff_call(*a, lambda_init, tq=tq_c, tiles=TILES["c_tiles_per_step_exact"],
                                  tk=TILES["c_keys_exact"], lagged=False),
            *c_args)
        yd = _nbr_call(tok3(dq), tok3(dk), tok3(dv), _nbr_bias_table(d_rel_bias[l]))
        xf = _merge_mlp_call(xf, ya.reshape(tokens, A_Q_COLS), obs, lbs, yc.reshape(tokens, C_V_COLS),
                             yd.reshape(tokens, D_COLS), attn_norm_g[l][None, :], mlp_norm_g[l][None, :],
                             w_l[:, QKV_COLS:], w_ba, w_branch_b[l].astype(BF16), w_branch_c[l].astype(BF16),
                             w_branch_d[l].astype(BF16), w_out[l].astype(BF16), w_up[l].astype(BF16),
                             w_down[l].astype(BF16), tm, TILES["merge_tiles_per_step"])
    return xf.reshape(b, seq, D_MODEL)
```

```python
import functools
import math

import jax
import jax.numpy as jnp
import numpy as np
from jax import lax
from jax.experimental import pallas as pl
from jax.experimental.pallas import tpu as pltpu

F32 = jnp.float32
BF16 = jnp.bfloat16

D_MODEL = 1024
HEAD_DIM = 64
GRID_W = 64
ROPE_THETA = 10000.0
NORM_EPS = 1e-6
NEG_INF = -1e30

A_Q_HEADS = 8
A_KV_HEADS = 2
A_RADIUS = 128
B_PAIRS = ((128, 1), (512, 4), (2048, 16))
B_HEADS_PER_GROUP = 2
B_HEADS = B_HEADS_PER_GROUP * len(B_PAIRS)
C_HEADS = 4
C_QK_DIM = 32
C_V_DIM = 2 * C_QK_DIM
D_HEADS = 4
NA_ROWS = 8
NA_COLS = 16
D_MLP = 4 * D_MODEL
N_BRANCHES = 4

A_Q_COLS = A_Q_HEADS * HEAD_DIM
A_KV_COLS = A_KV_HEADS * HEAD_DIM
B_COLS = B_HEADS * HEAD_DIM
C_QK_COLS = C_HEADS * 2 * C_QK_DIM
C_V_COLS = C_HEADS * C_V_DIM
D_COLS = D_HEADS * HEAD_DIM
GATE_COLS = N_BRANCHES * D_MODEL
QKV_COLS = (A_Q_COLS + 2 * A_KV_COLS + 3 * B_COLS + 2 * C_QK_COLS + C_V_COLS + 3 * D_COLS)

LANES = 128
MXU_COLS = 256
BF16_ROWS = 16
VMEM_LIMIT = 56 * 1024 * 1024
TILES = dict(proj_rows=512, proj_tiles_per_step=1, merge_rows=256, merge_tiles_per_step=2, a_queries=128,
             a_tiles_per_step=8, b_queries=128,
             c_queries=256, c_tiles_per_step=4, c_keys=256, c_tiles_per_step_exact=2, c_keys_exact=512)

_SECTIONS = (
    ("aq", A_Q_COLS, HEAD_DIM, True, BF16), ("ak", A_KV_COLS, HEAD_DIM, True, BF16), ("av", A_KV_COLS, None, False, BF16),
    ("bq", B_COLS, HEAD_DIM, True, F32), ("bk", B_COLS, HEAD_DIM, True, F32), ("bv", B_COLS, None, False, F32),
    ("cq", C_QK_COLS, C_QK_DIM, True, BF16), ("ck", C_QK_COLS, C_QK_DIM, True, BF16), ("cv", C_V_COLS, None, False, BF16),
    ("dq", D_COLS, HEAD_DIM, False, BF16), ("dk", D_COLS, HEAD_DIM, False, BF16), ("dv", D_COLS, None, False, BF16),
)
_SECTION_COL = {s[0]: sum(t[1] for t in _SECTIONS[:i]) for i, s in enumerate(_SECTIONS)}
_TRANSPOSED = ("av", "cv")
_TRANSPOSED_ROW = {n: sum(dict((s[0], s[1]) for s in _SECTIONS)[m] for m in _TRANSPOSED[:i])
                   for i, n in enumerate(_TRANSPOSED)}
_TRANSPOSED_ROWS = sum(s[1] for s in _SECTIONS if s[0] in _TRANSPOSED)
A_HEAD_ORDER = np.array([h for c in range(A_Q_HEADS // A_KV_HEADS) for h in (c, c + A_Q_HEADS // A_KV_HEADS)])
LOG2E = math.log2(math.e)
C_LAGGED_LOGIT_LIMIT = 16.0


def _nt_dot(a, b):
    return lax.dot_general(a, b, (((1,), (1,)), ((), ())), preferred_element_type=F32)


def _dot(a, b):
    return jnp.dot(a, b, preferred_element_type=F32)


def _rms_rows(x, g):
    ms = jnp.mean(x * x, axis=-1, keepdims=True)
    return x * lax.rsqrt(ms + NORM_EPS) * g


def _rotate_half(y, half):
    width = y.shape[1]
    lane = lax.broadcasted_iota(jnp.int32, y.shape, 1)
    first = (lane & (2 * half - 1)) < half
    return jnp.where(first, pltpu.roll(y, width - half, 1), pltpu.roll(y, half, 1))


def _proj_kernel(x_ref, g_ref, wqkv_ref, wt_ref, gains_ref, cos64_ref, sin64_ref, cos32_ref, sin32_ref,
                 e64_ref, e32_ref, *out_refs, rows, tiles):
    subs = [slice(s * rows, (s + 1) * rows) for s in range(tiles)]
    h = [_rms_rows(x_ref[sl, :], g_ref[...]).astype(BF16) for sl in subs]

    def project(job):
        idx, s = job
        name, width = _SECTIONS[idx][:2]
        if name in _TRANSPOSED:
            r0 = _TRANSPOSED_ROW[name]
            return _nt_dot(wt_ref[r0:r0 + width, :], h[s])
        return _dot(h[s], wqkv_ref[:, _SECTION_COL[name]:_SECTION_COL[name] + width])

    jobs = [(idx, s) for idx in range(len(_SECTIONS)) for s in range(tiles)]
    pending = [project(job) for job in jobs[:tiles]]
    for j, (idx, s) in enumerate(jobs):
        name, width, hd, rope, dtype = _SECTIONS[idx]
        o_ref, sl = out_refs[idx], subs[s]
        col = _SECTION_COL[name]
        t = pending.pop(0)
        if j + tiles < len(jobs):
            pending.append(project(jobs[j + tiles]))
        if name in _TRANSPOSED:
            o_ref[:, sl] = t.astype(dtype)
        elif hd is None:
            o_ref[sl, :] = t.astype(dtype)
        else:
            e_ref = e64_ref if hd == HEAD_DIM else e32_ref
            cos_ref, sin_ref = (cos64_ref, sin64_ref) if hd == HEAD_DIM else (cos32_ref, sin32_ref)
            for off in range(0, width, MXU_COLS):
                pw = min(MXU_COLS, width - off)
                tc = t[:, off:off + pw]
                ss = _dot((tc * tc).astype(BF16), e_ref[:pw, :pw])
                gain = gains_ref[:, col + off:col + off + pw]
                y = tc * lax.rsqrt(ss * (1.0 / hd) + NORM_EPS) * gain
                if rope:
                    y = y * cos_ref[sl, :pw] + _rotate_half(y, hd // 2) * sin_ref[sl, :pw]
                o_ref[sl, off:off + pw] = y.astype(dtype)


def _proj_call(x2d, g, wqkv, wt, gains, tabs, seq, tm, tiles):
    t_tokens = x2d.shape[0]
    n_seq_tiles = seq // tm
    const = lambda i: (0, 0)
    row = lambda i: (i, 0)
    pos = lambda i: (i % n_seq_tiles, 0)
    single = pl.Buffered(1)
    in_specs = [
        pl.BlockSpec((tm, D_MODEL), row),
        pl.BlockSpec((1, D_MODEL), const),
        pl.BlockSpec((D_MODEL, QKV_COLS), const, pipeline_mode=single),
        pl.BlockSpec((_TRANSPOSED_ROWS, D_MODEL), const, pipeline_mode=single),
        pl.BlockSpec((1, QKV_COLS), const),
        pl.BlockSpec((tm, MXU_COLS), pos), pl.BlockSpec((tm, MXU_COLS), pos),
        pl.BlockSpec((tm, MXU_COLS), pos), pl.BlockSpec((tm, MXU_COLS), pos),
        pl.BlockSpec((MXU_COLS, MXU_COLS), const), pl.BlockSpec((MXU_COLS, MXU_COLS), const),
    ]
    out_shape, out_specs = [], []
    for name, w, _, _, dtype in _SECTIONS:
        if name in _TRANSPOSED:
            out_shape.append(jax.ShapeDtypeStruct((w, t_tokens), dtype))
            out_specs.append(pl.BlockSpec((w, tm), lambda i: (0, i)))
        else:
            out_shape.append(jax.ShapeDtypeStruct((t_tokens, w), dtype))
            out_specs.append(pl.BlockSpec((tm, w), row))
    return pl.pallas_call(
        functools.partial(_proj_kernel, rows=tm // tiles, tiles=tiles), grid=(t_tokens // tm,),
        in_specs=in_specs, out_specs=out_specs, out_shape=out_shape,
        compiler_params=pltpu.CompilerParams(dimension_semantics=("arbitrary",), vmem_limit_bytes=VMEM_LIMIT),
        name="proj",
    )(x2d, g, wqkv, wt, gains, tabs["cos64"], tabs["sin64"], tabs["cos32"], tabs["sin32"],
      tabs["e64"], tabs["e32"])


def _band_bias(n, tq, win, radius, start):
    qpos = n * tq + lax.broadcasted_iota(jnp.int32, (tq, win), 0)
    kpos = start + lax.broadcasted_iota(jnp.int32, (tq, win), 1)
    return jnp.where(jnp.abs(kpos - qpos) <= radius, 0.0, NEG_INF).astype(F32)


def _window_kernel(sink_ref, q_ref, k_ref, vt_ref, o_ref, *, tq, tiles, seq, radius):
    win = tq + 2 * radius
    left = lax.broadcasted_iota(jnp.int32, (tq, LANES), 1) < HEAD_DIM
    first = lax.broadcasted_iota(jnp.int32, (1, 2 * tq), 1) < tq
    rep = A_Q_HEADS // A_KV_HEADS
    ones = jnp.ones((BF16_ROWS, win), BF16)

    def window(sub):
        n = pl.program_id(1) * tiles + sub
        start = pl.multiple_of(jnp.clip(n * tq - radius, 0, seq - win), radius)
        kpos = start + lax.broadcasted_iota(jnp.int32, (win, tq), 0)
        qpos = n * tq + lax.broadcasted_iota(jnp.int32, (win, tq), 1)
        bias = jnp.where(jnp.abs(kpos - qpos) <= radius, 0.0, NEG_INF).astype(F32)
        return (k_ref[0, pl.ds(start, win), :], jnp.concatenate([vt_ref[:, pl.ds(start, win)], ones], axis=0),
                jnp.concatenate([bias, bias], axis=1))

    windows = [window(sub) for sub in range(tiles)]
    jobs = [(sub, c) for sub in range(tiles) for c in range(rep)]

    def scores(job):
        sub, c = job
        qc = q_ref[0, sub * tq:(sub + 1) * tq, c * LANES:(c + 1) * LANES]
        zero = jnp.zeros_like(qc)
        qm = jnp.concatenate([jnp.where(left, qc, zero), jnp.where(left, zero, qc)], axis=0)
        return _nt_dot(windows[sub][0], qm) + windows[sub][2]

    ahead = 4
    pending = [scores(job) for job in jobs[:ahead]]
    for j, (sub, c) in enumerate(jobs):
        st = pending.pop(0)
        if j + ahead < len(jobs):
            pending.append(scores(jobs[j + ahead]))
        sk = jnp.where(first, sink_ref[c], sink_ref[c + rep]) * LOG2E
        m = jnp.maximum(jnp.max(st, axis=0, keepdims=True), sk)
        pt = jnp.exp2(st - m).astype(BF16)
        acc = _dot(windows[sub][1], pt)
        den = acc[LANES:LANES + 1] + jnp.exp2(sk - m)
        ot = jnp.concatenate([acc[:HEAD_DIM, :tq] / den[:, :tq], acc[HEAD_DIM:LANES, tq:] / den[:, tq:]], axis=0)
        o_ref[0, sub * tq:(sub + 1) * tq, c * LANES:(c + 1) * LANES] = ot.T.astype(BF16)


def _window_call(sink, q, k, vt, tq, tiles):
    b, seq, _ = q.shape
    kern = functools.partial(_window_kernel, tq=tq, tiles=tiles, seq=seq, radius=A_RADIUS)
    return pl.pallas_call(
        kern, grid=(b, seq // (tq * tiles)),
        in_specs=[pl.BlockSpec(memory_space=pltpu.SMEM),
                  pl.BlockSpec((1, tq * tiles, A_Q_COLS), lambda i, n: (i, n, 0)),
                  pl.BlockSpec((1, seq, A_KV_COLS), lambda i, n: (i, 0, 0)),
                  pl.BlockSpec((A_KV_COLS, seq), lambda i, n: (0, i))],
        out_specs=pl.BlockSpec((1, tq * tiles, A_Q_COLS), lambda i, n: (i, n, 0)),
        out_shape=jax.ShapeDtypeStruct((b, seq, A_Q_COLS), BF16),
        compiler_params=pltpu.CompilerParams(dimension_semantics=("arbitrary", "arbitrary")),
        name="mixer_a",
    )(sink, q, k, vt)


def _dilated_kernel(q_ref, k_ref, v_ref, o_ref, lse_ref, *, dil, tq, seq, radius):
    sub_len = seq // dil
    n_tiles = sub_len // tq
    win = tq + 2 * radius

    def rows(first, count):
        if dil == 1:
            return pl.ds(pl.multiple_of(first, radius), count)
        return pl.ds(first, count, stride=dil)

    left = lax.broadcasted_iota(jnp.int32, (tq, LANES), 1) < HEAD_DIM
    ones = jnp.ones((win, LANES), BF16)

    blocks_per_iter = 8
    shared_window = win == sub_len and blocks_per_iter % n_tiles == 0

    def load_kv(r, start):
        kv_rows = rows(r + dil * start, win)
        return k_ref[0, kv_rows, :].astype(BF16), v_ref[0, kv_rows, :].astype(BF16)

    def scores(it, kv=None):
        r = it // n_tiles
        n = it - r * n_tiles
        start = jnp.clip(n * tq - radius, 0, sub_len - win)
        q_rows = rows(r + dil * (n * tq), tq)
        q = q_ref[0, q_rows, :].astype(BF16)
        k, v = load_kv(r, start) if kv is None else kv
        zero = jnp.zeros_like(q)
        q2 = jnp.concatenate([jnp.where(left, q, zero), jnp.where(left, zero, q)], axis=0)
        bias = _band_bias(n, tq, win, radius, start)
        return q_rows, v, _nt_dot(q2, k) + jnp.concatenate([bias, bias], axis=0)

    def finish(q_rows, v, s):
        m = jnp.max(s, axis=-1, keepdims=True)
        p = jnp.exp(s - m).astype(BF16)
        acc = _dot(p, jnp.concatenate([v, ones], axis=1))
        den = acc[:, LANES:]
        o2 = acc[:, :LANES] / den
        lse2 = jnp.log(den) + m
        o_ref[0, q_rows, :] = jnp.where(left, o2[:tq], o2[tq:])
        lse_ref[0, q_rows, :] = jnp.where(left, lse2[:tq], lse2[tq:])

    def body(j, carry):
        pending = []
        for u in range(blocks_per_iter):
            it = blocks_per_iter * j + u
            if shared_window and u % n_tiles == 0:
                kv = load_kv(it // n_tiles, 0)
            pending.append(scores(it, kv if shared_window else None))
        for blk in pending:
            finish(*blk)
        return carry

    lax.fori_loop(0, dil * n_tiles // blocks_per_iter, body, 0)


def _dilated_call(q, k, v, group, window, dil, tq):
    b, seq, _ = q.shape
    radius = window // (2 * dil)
    kern = functools.partial(_dilated_kernel, dil=dil, tq=tq, seq=seq, radius=radius)
    out_sds = jax.ShapeDtypeStruct((b, seq, LANES), F32)
    chunk = lambda i: (i, 0, group)
    whole = lambda i: (i, 0, 0)
    return pl.pallas_call(
        kern, grid=(b,),
        in_specs=[pl.BlockSpec((1, seq, LANES), chunk)] * 3,
        out_specs=[pl.BlockSpec((1, seq, LANES), whole)] * 2,
        out_shape=[out_sds, out_sds],
        compiler_params=pltpu.CompilerParams(dimension_semantics=("arbitrary",)),
        name=f"mixer_b_d{dil}",
    )(q, k, v)


def _diff_kernel(q_ref, k_ref, vt_ref, lam_ref, subg_ref, o_ref, *, tq, tiles, tk, seq, lambda_init, lagged):
    nseg = LANES // C_QK_DIM
    lane = lax.broadcasted_iota(jnp.int32, (tq, LANES), 1)

    def masked_rows(sub):
        q = q_ref[0, sub * tq:(sub + 1) * tq, :]
        zero = jnp.zeros_like(q)
        return jnp.concatenate([jnp.where((lane // C_QK_DIM) == j, q, zero) for j in range(nseg)], axis=0)

    q4 = [masked_rows(sub) for sub in range(tiles)]
    ones_rows = jnp.ones((BF16_ROWS, tk), BF16)
    m = [jnp.full((1, nseg * tq), NEG_INF, F32)] * tiles
    ref = list(m)
    acc = [jnp.zeros((LANES + BF16_ROWS, nseg * tq), F32)] * tiles
    n_blocks = seq // tk
    jobs = [(t, sub) for t in range(n_blocks) for sub in range(tiles)]
    scores = lambda job: _nt_dot(k_ref[0, job[0] * tk:(job[0] + 1) * tk, :], q4[job[1]])
    ahead = (1 if lagged else 2) * tiles
    pending = [scores(job) for job in jobs[:ahead]]
    for j, (t, sub) in enumerate(jobs):
        st = pending.pop(0)
        if j + ahead < len(jobs):
            pending.append(scores(jobs[j + ahead]))
        colmax = jnp.max(st, axis=0, keepdims=True)
        ref_new = m[sub] if (lagged and t > 0) else jnp.maximum(m[sub], colmax)
        alpha = jnp.exp2(ref[sub] - ref_new)
        pt = jnp.exp2(st - ref_new).astype(BF16)
        vext = jnp.concatenate([vt_ref[:, t * tk:(t + 1) * tk], ones_rows], axis=0)
        acc[sub] = acc[sub] * alpha + _dot(vext, pt)
        ref[sub] = ref_new
        m[sub] = jnp.maximum(m[sub], colmax)

    lp = lam_ref[...]
    lam = (jnp.exp(jnp.sum(lp[0:1] * lp[1:2], axis=-1, keepdims=True))
           - jnp.exp(jnp.sum(lp[2:3] * lp[3:4], axis=-1, keepdims=True)) + lambda_init)
    for sub in range(tiles):
        heads = []
        for hh in range(LANES // C_V_DIM):
            u = []
            for c in range(2):
                cols = slice((2 * hh + c) * tq, (2 * hh + c + 1) * tq)
                u.append(acc[sub][hh * C_V_DIM:(hh + 1) * C_V_DIM, cols] / acc[sub][LANES:LANES + 1, cols])
            o = u[0] - lam * u[1]
            ss = jnp.sum(o * o, axis=0, keepdims=True)
            heads.append(o * lax.rsqrt(ss * (1.0 / C_V_DIM) + NORM_EPS))
        ot = jnp.concatenate(heads, axis=0) * subg_ref[...] * (1.0 - lambda_init)
        o_ref[0, sub * tq:(sub + 1) * tq, :] = ot.T.astype(BF16)


def _diff_call(q, k, vt, lam_p, subg, lambda_init, tq, tiles, tk, lagged):
    b, seq, _ = q.shape
    pairs = C_QK_COLS // LANES
    kern = functools.partial(_diff_kernel, tq=tq, tiles=tiles, tk=tk, seq=seq, lambda_init=lambda_init,
                             lagged=lagged)
    return pl.pallas_call(
        kern, grid=(b, pairs, seq // (tq * tiles)),
        in_specs=[pl.BlockSpec((1, tq * tiles, LANES), lambda i, p, n: (i, n, p)),
                  pl.BlockSpec((1, seq, LANES), lambda i, p, n: (i, 0, p)),
                  pl.BlockSpec((LANES, seq), lambda i, p, n: (p, i)),
                  pl.BlockSpec((4, C_QK_DIM), lambda i, p, n: (0, 0)),
                  pl.BlockSpec((LANES, tq), lambda i, p, n: (0, 0))],
        out_specs=pl.BlockSpec((1, tq * tiles, LANES), lambda i, p, n: (i, n, p)),
        out_shape=jax.ShapeDtypeStruct((b, seq, C_V_COLS), BF16),
        compiler_params=pltpu.CompilerParams(dimension_semantics=("arbitrary",) * 3),
        name="mixer_c_lagged" if lagged else "mixer_c",
    )(q, k, vt, lam_p, subg)


def _nbr_kernel(q_ref, k_ref, v_ref, bias_ref, o_ref, *, rows):
    kr = NA_ROWS
    lane = lax.broadcasted_iota(jnp.int32, (GRID_W, LANES), 1)
    left = lane < HEAD_DIM

    ones = jnp.ones((kr * GRID_W, LANES), BF16)

    def scores(i):
        r0 = jnp.clip(i - kr // 2, 0, rows - kr)
        cls = jnp.minimum(i, kr // 2) + jnp.maximum(i - (rows - kr // 2), 0)
        q = q_ref[0, pl.ds(pl.multiple_of(i * GRID_W, GRID_W), GRID_W), :]
        koff = pl.multiple_of(r0 * GRID_W, GRID_W)
        k = k_ref[0, pl.ds(koff, kr * GRID_W), :]
        v = v_ref[0, pl.ds(koff, kr * GRID_W), :]
        zero = jnp.zeros_like(q)
        q2 = jnp.concatenate([jnp.where(left, q, zero), jnp.where(left, zero, q)], axis=0)
        bias = jnp.concatenate(
            [jnp.concatenate([bias_ref[hh, NA_ROWS - 1 - cls + 2 * u] for u in range(kr // 2)], axis=1)
             for hh in range(2)], axis=0)
        return i, v, _nt_dot(q2, k) + bias

    def finish(i, v, s):
        m = jnp.max(s, axis=-1, keepdims=True)
        p = jnp.exp(s - m).astype(BF16)
        acc = _dot(p, jnp.concatenate([v, ones], axis=1))
        pv = acc[:, :LANES] / acc[:, LANES:]
        o = jnp.where(left, pv[:GRID_W], pv[GRID_W:])
        o_ref[0, pl.ds(pl.multiple_of(i * GRID_W, GRID_W), GRID_W), :] = o.astype(BF16)

    rows_per_iter = 16

    def body(j, carry):
        pending = [scores(rows_per_iter * j + u) for u in range(rows_per_iter)]
        for blk in pending:
            finish(*blk)
        return carry

    lax.fori_loop(0, rows // rows_per_iter, body, 0)


def _nbr_call(q, k, v, bias_tab):
    b, seq, _ = q.shape
    rows = seq // GRID_W
    pairs = D_COLS // LANES
    kern = functools.partial(_nbr_kernel, rows=rows)
    blk = lambda i, p: (i, 0, p)
    return pl.pallas_call(
        kern, grid=(b, pairs),
        in_specs=[pl.BlockSpec((1, seq, LANES), blk), pl.BlockSpec((1, seq, LANES), blk),
                  pl.BlockSpec((1, seq, LANES), blk),
                  pl.BlockSpec((2,) + bias_tab.shape[1:], lambda i, p: (p, 0, 0, 0))],
        out_specs=pl.BlockSpec((1, seq, LANES), blk),
        out_shape=jax.ShapeDtypeStruct((b, seq, D_COLS), BF16),
        compiler_params=pltpu.CompilerParams(dimension_semantics=("arbitrary",) * 2),
        name="mixer_d",
    )(q, k, v, bias_tab)


def _nbr_bias_table(rpb):
    cj = np.arange(GRID_W)
    c0 = np.clip(cj - NA_COLS // 2, 0, GRID_W - NA_COLS)
    col_ok = (cj[None, :] >= c0[:, None]) & (cj[None, :] < c0[:, None] + NA_COLS)
    dc = np.clip(cj[None, :] - cj[:, None], -(NA_COLS - 1), NA_COLS - 1) + (NA_COLS - 1)
    onehot = jnp.asarray(np.arange(2 * NA_COLS - 1)[:, None, None] == dc[None], dtype=F32)
    per_row = jnp.einsum('hrj,jqk->hrqk', rpb.astype(F32), onehot, precision=lax.Precision.HIGHEST)
    per_row = jnp.where(col_ok[None, None], per_row, NEG_INF)
    return jnp.concatenate([per_row[:, :-1], per_row[:, 1:]], axis=-1)


def _merge_mlp_kernel(x_ref, ya_ref, ob1_ref, ob2_ref, ob3_ref, lb1_ref, lb2_ref, lb3_ref, yc_ref, yd_ref,
                      ga_ref, gm_ref, wg_ref, wa_ref, wb_ref, wc_ref, wd_ref, wo_ref, wu_ref, wdn_ref, o_ref,
                      *, rows, tiles):
    subs = [slice(s * rows, (s + 1) * rows) for s in range(tiles)]
    weights = (wa_ref, wb_ref, wc_ref, wd_ref)
    xs, merged = [], []
    for sl in subs:
        x = x_ref[sl, :]
        h = _rms_rows(x, ga_ref[...]).astype(BF16)
        l1, l2, l3 = lb1_ref[sl, :], lb2_ref[sl, :], lb3_ref[sl, :]
        m = jnp.maximum(jnp.maximum(l1, l2), l3)
        w1, w2, w3 = jnp.exp(l1 - m), jnp.exp(l2 - m), jnp.exp(l3 - m)
        yb = (w1 * ob1_ref[sl, :] + w2 * ob2_ref[sl, :] + w3 * ob3_ref[sl, :]) / (w1 + w2 + w3)
        branches = (ya_ref[sl, :], yb.astype(BF16), yc_ref[sl, :], yd_ref[sl, :])
        total = None
        for c in range(N_BRANCHES):
            z = _dot(h, wg_ref[:, c * D_MODEL:(c + 1) * D_MODEL])
            term = _dot(branches[c], weights[c][...]) / (1.0 + jnp.exp(-z))
            total = term if total is None else total + term
        xs.append(x)
        merged.append(total)
    x1 = [x + _dot(t.astype(BF16), wo_ref[...]) for x, t in zip(xs, merged)]
    hm = [_rms_rows(v, gm_ref[...]).astype(BF16) for v in x1]
    acc = list(x1)
    n_chunks = D_MLP // D_MODEL
    jobs = [(c, s) for c in range(n_chunks) for s in range(tiles)]
    up = lambda job: _dot(hm[job[1]], wu_ref[:, job[0] * D_MODEL:(job[0] + 1) * D_MODEL])
    pending = [up(job) for job in jobs[:tiles]]
    for j, (c, s) in enumerate(jobs):
        u = pending.pop(0)
        if j + tiles < len(jobs):
            pending.append(up(jobs[j + tiles]))
        u = jnp.square(jnp.maximum(u, 0.0)).astype(BF16)
        acc[s] = acc[s] + _dot(u, wdn_ref[c * D_MODEL:(c + 1) * D_MODEL, :])
    for s, sl in enumerate(subs):
        o_ref[sl, :] = acc[s]


def _merge_mlp_call(x2d, ya, obs, lbs, yc, yd, ga, gm, wg, wa, wb, wc, wd, wo, wu, wdn, tm, tiles):
    t_tokens = x2d.shape[0]
    row = lambda i: (i, 0)
    const = lambda i: (0, 0)
    single = pl.Buffered(1)
    acts = [x2d, ya, *obs, *lbs, yc, yd]
    consts = [ga, gm, wg, wa, wb, wc, wd, wo, wu, wdn]
    in_specs = ([pl.BlockSpec((tm * tiles, a.shape[1]), row) for a in acts]
                + [pl.BlockSpec(w.shape, const, pipeline_mode=single) for w in consts])
    return pl.pallas_call(
        functools.partial(_merge_mlp_kernel, rows=tm, tiles=tiles), grid=(t_tokens // (tm * tiles),),
        in_specs=in_specs, out_specs=pl.BlockSpec((tm * tiles, D_MODEL), row),
        out_shape=jax.ShapeDtypeStruct((t_tokens, D_MODEL), F32),
        compiler_params=pltpu.CompilerParams(dimension_semantics=("arbitrary",), vmem_limit_bytes=VMEM_LIMIT),
        name="merge_mlp",
    )(*acts, *consts)


def _rope_lane_tables(seq, dim):
    inv = ROPE_THETA ** (-jnp.arange(0, dim, 2, dtype=F32) / dim)
    ang = jnp.arange(seq, dtype=F32)[:, None] * inv[None, :]
    cos, sin = jnp.cos(ang), jnp.sin(ang)
    reps = MXU_COLS // dim
    return (jnp.tile(jnp.concatenate([cos, cos], axis=-1), (1, reps)),
            jnp.tile(jnp.concatenate([-sin, sin], axis=-1), (1, reps)))


def _blockdiag_ones(seg):
    idx = np.arange(MXU_COLS) // seg
    return jnp.asarray(idx[:, None] == idx[None, :], dtype=BF16)


def _gain_row(l, a_g, b_g, c_g, d_g):
    ones = lambda n: jnp.ones((n,), F32)
    s64, s32 = HEAD_DIM ** -0.5, C_QK_DIM ** -0.5
    parts = [
        jnp.tile(a_g[l, 0] * (s64 * LOG2E), A_Q_HEADS), jnp.tile(a_g[l, 1], A_KV_HEADS), ones(A_KV_COLS),
        jnp.tile(b_g[l, 0] * s64, B_HEADS), jnp.tile(b_g[l, 1], B_HEADS), ones(B_COLS),
        jnp.tile(c_g[l, 0] * (s32 * LOG2E), 2 * C_HEADS), jnp.tile(c_g[l, 1], 2 * C_HEADS), ones(C_V_COLS),
        jnp.tile(d_g[l, 0] * s64, D_HEADS), jnp.tile(d_g[l, 1], D_HEADS), ones(D_COLS),
    ]
    return jnp.concatenate(parts).astype(F32)[None, :]


def kernel(x, attn_norm_g, w_in, a_qk_norm_g, a_sink, b_qk_norm_g, c_qk_norm_g, c_lambda, c_subln_g,
           d_qk_norm_g, d_rel_bias, w_branch_a, w_branch_b, w_branch_c, w_branch_d, w_out, mlp_norm_g,
           w_up, w_down):
    b, seq, _ = x.shape
    depth = w_in.shape[0]
    tokens = b * seq
    tm, tm_proj, tq_c = TILES["merge_rows"], TILES["proj_rows"], TILES["c_queries"]
    cos64, sin64 = _rope_lane_tables(seq, HEAD_DIM)
    cos32, sin32 = _rope_lane_tables(seq, C_QK_DIM)
    tabs = dict(cos64=cos64, sin64=sin64, cos32=cos32, sin32=sin32,
                e64=_blockdiag_ones(HEAD_DIM), e32=_blockdiag_ones(C_QK_DIM))
    xf = x.reshape(tokens, D_MODEL)
    for l in range(depth):
        w_l = w_in[l].astype(BF16)
        gains = _gain_row(l, a_qk_norm_g, b_qk_norm_g, c_qk_norm_g, d_qk_norm_g)
        w_aq = w_l[:, :A_Q_COLS].reshape(D_MODEL, A_Q_HEADS, HEAD_DIM)[:, A_HEAD_ORDER, :].reshape(D_MODEL, A_Q_COLS)
        wqkv = jnp.concatenate([w_aq, w_l[:, A_Q_COLS:QKV_COLS]], axis=1)
        wt = jnp.concatenate([w_l[:, _SECTION_COL[n]:_SECTION_COL[n] + dict((s[0], s[1]) for s in _SECTIONS)[n]].T
                              for n in _TRANSPOSED], axis=0)
        w_ba = (w_branch_a[l].astype(BF16).reshape(A_Q_HEADS, HEAD_DIM, D_MODEL)[A_HEAD_ORDER, :, :]
                .reshape(A_Q_COLS, D_MODEL))
        outs = _proj_call(xf, attn_norm_g[l][None, :], wqkv, wt, gains, tabs, seq, tm_proj,
                          TILES["proj_tiles_per_step"])
        aq, ak, avt, bq, bk, bv, cq, ck, cvt, dq, dk, dv = outs
        tok3 = lambda t: t.reshape(b, seq, t.shape[-1])
        ya = _window_call(a_sink[l].astype(F32), tok3(aq), tok3(ak), avt, tq=TILES["a_queries"],
                          tiles=TILES["a_tiles_per_step"])
        obs, lbs = [], []
        for gi, (window, dil) in enumerate(B_PAIRS):
            o, lse = _dilated_call(tok3(bq), tok3(bk), tok3(bv), gi, window, dil, tq=TILES["b_queries"])
            obs.append(o.reshape(tokens, LANES))
            lbs.append(lse.reshape(tokens, LANES))
        lambda_init = 0.8 - 0.6 * math.exp(-0.3 * l)
        subg = jnp.broadcast_to(jnp.tile(c_subln_g[l].astype(F32), LANES // C_V_DIM)[:, None], (LANES, tq_c))
        logit_bound = (C_QK_DIM ** 0.5 * LOG2E) * jnp.max(jnp.abs(c_qk_norm_g[l, 0])) * jnp.max(jnp.abs(c_qk_norm_g[l, 1]))
        c_args = (tok3(cq), tok3(ck), cvt, c_lambda[l].astype(F32), subg)
        yc = lax.cond(
            logit_bound < C_LAGGED_LOGIT_LIMIT,
            lambda *a: _diff_call(*a, lambda_init, tq=tq_c, tiles=TILES["c_tiles_per_step"], tk=TILES["c_keys"],
                                  lagged=True),
            lambda *a: _diff_call(*a, lambda_init, tq=tq_c, tiles=TILES["c_tiles_per_step_exact"],
                                  tk=TILES["c_keys_exact"], lagged=False),
            *c_args)
        yd = _nbr_call(tok3(dq), tok3(dk), tok3(dv), _nbr_bias_table(d_rel_bias[l]))
        xf = _merge_mlp_call(xf, ya.reshape(tokens, A_Q_COLS), obs, lbs, yc.reshape(tokens, C_V_COLS),
                             yd.reshape(tokens, D_COLS), attn_norm_g[l][None, :], mlp_norm_g[l][None, :],
                             w_l[:, QKV_COLS:], w_ba, w_branch_b[l].astype(BF16), w_branch_c[l].astype(BF16),
                             w_branch_d[l].astype(BF16), w_out[l].astype(BF16), w_up[l].astype(BF16),
                             w_down[l].astype(BF16), tm, TILES["merge_tiles_per_step"])
    return xf.reshape(b, seq, D_MODEL)
```

```python
import functools
import math

import jax
import jax.numpy as jnp
import numpy as np
from jax import lax
from jax.experimental import pallas as pl
from jax.experimental.pallas import tpu as pltpu

F32 = jnp.float32
BF16 = jnp.bfloat16

D_MODEL = 1024
HEAD_DIM = 64
GRID_W = 64
ROPE_THETA = 10000.0
NORM_EPS = 1e-6
NEG_INF = -1e30

A_Q_HEADS = 8
A_KV_HEADS = 2
A_RADIUS = 128
B_PAIRS = ((128, 1), (512, 4), (2048, 16))
B_HEADS_PER_GROUP = 2
B_HEADS = B_HEADS_PER_GROUP * len(B_PAIRS)
C_HEADS = 4
C_QK_DIM = 32
C_V_DIM = 2 * C_QK_DIM
D_HEADS = 4
NA_ROWS = 8
NA_COLS = 16
D_MLP = 4 * D_MODEL
N_BRANCHES = 4

A_Q_COLS = A_Q_HEADS * HEAD_DIM
A_KV_COLS = A_KV_HEADS * HEAD_DIM
B_COLS = B_HEADS * HEAD_DIM
C_QK_COLS = C_HEADS * 2 * C_QK_DIM
C_V_COLS = C_HEADS * C_V_DIM
D_COLS = D_HEADS * HEAD_DIM
GATE_COLS = N_BRANCHES * D_MODEL
QKV_COLS = (A_Q_COLS + 2 * A_KV_COLS + 3 * B_COLS + 2 * C_QK_COLS + C_V_COLS + 3 * D_COLS)

LANES = 128
MXU_COLS = 256
BF16_ROWS = 16
VMEM_LIMIT = 56 * 1024 * 1024
TILES = dict(proj_rows=512, proj_tiles_per_step=1, merge_rows=256, merge_tiles_per_step=2, a_queries=128,
             a_tiles_per_step=8, b_queries=128,
             c_queries=256, c_tiles_per_step=4, c_keys=256, c_tiles_per_step_exact=2, c_keys_exact=512)

_SECTIONS = (
    ("aq", A_Q_COLS, HEAD_DIM, True, BF16), ("ak", A_KV_COLS, HEAD_DIM, True, BF16), ("av", A_KV_COLS, None, False, BF16),
    ("bq", B_COLS, HEAD_DIM, True, F32), ("bk", B_COLS, HEAD_DIM, True, F32), ("bv", B_COLS, None, False, F32),
    ("cq", C_QK_COLS, C_QK_DIM, True, BF16), ("ck", C_QK_COLS, C_QK_DIM, True, BF16), ("cv", C_V_COLS, None, False, BF16),
    ("dq", D_COLS, HEAD_DIM, False, BF16), ("dk", D_COLS, HEAD_DIM, False, BF16), ("dv", D_COLS, None, False, BF16),
)
_SECTION_COL = {s[0]: sum(t[1] for t in _SECTIONS[:i]) for i, s in enumerate(_SECTIONS)}
_TRANSPOSED = ("av", "cv")
_TRANSPOSED_ROW = {n: sum(dict((s[0], s[1]) for s in _SECTIONS)[m] for m in _TRANSPOSED[:i])
                   for i, n in enumerate(_TRANSPOSED)}
_TRANSPOSED_ROWS = sum(s[1] for s in _SECTIONS if s[0] in _TRANSPOSED)
A_HEAD_ORDER = np.array([h for c in range(A_Q_HEADS // A_KV_HEADS) for h in (c, c + A_Q_HEADS // A_KV_HEADS)])
LOG2E = math.log2(math.e)
C_LAGGED_LOGIT_LIMIT = 16.0


def _nt_dot(a, b):
    return lax.dot_general(a, b, (((1,), (1,)), ((), ())), preferred_element_type=F32)


def _dot(a, b):
    return jnp.dot(a, b, preferred_element_type=F32)


def _rms_rows(x, g):
    ms = jnp.mean(x * x, axis=-1, keepdims=True)
    return x * lax.rsqrt(ms + NORM_EPS) * g


def _rotate_half(y, half):
    width = y.shape[1]
    lane = lax.broadcasted_iota(jnp.int32, y.shape, 1)
    first = (lane & (2 * half - 1)) < half
    return jnp.where(first, pltpu.roll(y, width - half, 1), pltpu.roll(y, half, 1))


def _proj_kernel(x_ref, g_ref, wqkv_ref, wt_ref, gains_ref, cos64_ref, sin64_ref, cos32_ref, sin32_ref,
                 e64_ref, e32_ref, *out_refs, rows, tiles):
    subs = [slice(s * rows, (s + 1) * rows) for s in range(tiles)]
    h = [_rms_rows(x_ref[sl, :], g_ref[...]).astype(BF16) for sl in subs]

    def project(job):
        idx, s = job
        name, width = _SECTIONS[idx][:2]
        if name in _TRANSPOSED:
            r0 = _TRANSPOSED_ROW[name]
            return _nt_dot(wt_ref[r0:r0 + width, :], h[s])
        return _dot(h[s], wqkv_ref[:, _SECTION_COL[name]:_SECTION_COL[name] + width])

    jobs = [(idx, s) for idx in range(len(_SECTIONS)) for s in range(tiles)]
    pending = [project(job) for job in jobs[:tiles]]
    for j, (idx, s) in enumerate(jobs):
        name, width, hd, rope, dtype = _SECTIONS[idx]
        o_ref, sl = out_refs[idx], subs[s]
        col = _SECTION_COL[name]
        t = pending.pop(0)
        if j + tiles < len(jobs):
            pending.append(project(jobs[j + tiles]))
        if name in _TRANSPOSED:
            o_ref[:, sl] = t.astype(dtype)
        elif hd is None:
            o_ref[sl, :] = t.astype(dtype)
        else:
            e_ref = e64_ref if hd == HEAD_DIM else e32_ref
            cos_ref, sin_ref = (cos64_ref, sin64_ref) if hd == HEAD_DIM else (cos32_ref, sin32_ref)
            for off in range(0, width, MXU_COLS):
                pw = min(MXU_COLS, width - off)
                tc = t[:, off:off + pw]
                ss = _dot((tc * tc).astype(BF16), e_ref[:pw, :pw])
                gain = gains_ref[:, col + off:col + off + pw]
                y = tc * lax.rsqrt(ss * (1.0 / hd) + NORM_EPS) * gain
                if rope:
                    y = y * cos_ref[sl, :pw] + _rotate_half(y, hd // 2) * sin_ref[sl, :pw]
                o_ref[sl, off:off + pw] = y.astype(dtype)


def _proj_call(x2d, g, wqkv, wt, gains, tabs, seq, tm, tiles):
    t_tokens = x2d.shape[0]
    n_seq_tiles = seq // tm
    const = lambda i: (0, 0)
    row = lambda i: (i, 0)
    pos = lambda i: (i % n_seq_tiles, 0)
    single = pl.Buffered(1)
    in_specs = [
        pl.BlockSpec((tm, D_MODEL), row),
        pl.BlockSpec((1, D_MODEL), const),
        pl.BlockSpec((D_MODEL, QKV_COLS), const, pipeline_mode=single),
        pl.BlockSpec((_TRANSPOSED_ROWS, D_MODEL), const, pipeline_mode=single),
        pl.BlockSpec((1, QKV_COLS), const),
        pl.BlockSpec((tm, MXU_COLS), pos), pl.BlockSpec((tm, MXU_COLS), pos),
        pl.BlockSpec((tm, MXU_COLS), pos), pl.BlockSpec((tm, MXU_COLS), pos),
        pl.BlockSpec((MXU_COLS, MXU_COLS), const), pl.BlockSpec((MXU_COLS, MXU_COLS), const),
    ]
    out_shape, out_specs = [], []
    for name, w, _, _, dtype in _SECTIONS:
        if name in _TRANSPOSED:
            out_shape.append(jax.ShapeDtypeStruct((w, t_tokens), dtype))
            out_specs.append(pl.BlockSpec((w, tm), lambda i: (0, i)))
        else:
            out_shape.append(jax.ShapeDtypeStruct((t_tokens, w), dtype))
            out_specs.append(pl.BlockSpec((tm, w), row))
    return pl.pallas_call(
        functools.partial(_proj_kernel, rows=tm // tiles, tiles=tiles), grid=(t_tokens // tm,),
        in_specs=in_specs, out_specs=out_specs, out_shape=out_shape,
        compiler_params=pltpu.CompilerParams(dimension_semantics=("arbitrary",), vmem_limit_bytes=VMEM_LIMIT),
        name="proj",
    )(x2d, g, wqkv, wt, gains, tabs["cos64"], tabs["sin64"], tabs["cos32"], tabs["sin32"],
      tabs["e64"], tabs["e32"])


def _band_bias(n, tq, win, radius, start):
    qpos = n * tq + lax.broadcasted_iota(jnp.int32, (tq, win), 0)
    kpos = start + lax.broadcasted_iota(jnp.int32, (tq, win), 1)
    return jnp.where(jnp.abs(kpos - qpos) <= radius, 0.0, NEG_INF).astype(F32)


def _window_kernel(sink_ref, q_ref, k_ref, vt_ref, o_ref, *, tq, tiles, seq, radius):
    win = tq + 2 * radius
    left = lax.broadcasted_iota(jnp.int32, (tq, LANES), 1) < HEAD_DIM
    first = lax.broadcasted_iota(jnp.int32, (1, 2 * tq), 1) < tq
    rep = A_Q_HEADS // A_KV_HEADS
    ones = jnp.ones((BF16_ROWS, win), BF16)

    def window(sub):
        n = pl.program_id(1) * tiles + sub
        start = pl.multiple_of(jnp.clip(n * tq - radius, 0, seq - win), radius)
        kpos = start + lax.broadcasted_iota(jnp.int32, (win, tq), 0)
        qpos = n * tq + lax.broadcasted_iota(jnp.int32, (win, tq), 1)
        bias = jnp.where(jnp.abs(kpos - qpos) <= radius, 0.0, NEG_INF).astype(F32)
        return (k_ref[0, pl.ds(start, win), :], jnp.concatenate([vt_ref[:, pl.ds(start, win)], ones], axis=0),
                jnp.concatenate([bias, bias], axis=1))

    windows = [window(sub) for sub in range(tiles)]
    jobs = [(sub, c) for sub in range(tiles) for c in range(rep)]

    def scores(job):
        sub, c = job
        qc = q_ref[0, sub * tq:(sub + 1) * tq, c * LANES:(c + 1) * LANES]
        zero = jnp.zeros_like(qc)
        qm = jnp.concatenate([jnp.where(left, qc, zero), jnp.where(left, zero, qc)], axis=0)
        return _nt_dot(windows[sub][0], qm) + windows[sub][2]

    ahead = 4
    pending = [scores(job) for job in jobs[:ahead]]
    for j, (sub, c) in enumerate(jobs):
        st = pending.pop(0)
        if j + ahead < len(jobs):
            pending.append(scores(jobs[j + ahead]))
        sk = jnp.where(first, sink_ref[c], sink_ref[c + rep]) * LOG2E
        m = jnp.maximum(jnp.max(st, axis=0, keepdims=True), sk)
        pt = jnp.exp2(st - m).astype(BF16)
        acc = _dot(windows[sub][1], pt)
        den = acc[LANES:LANES + 1] + jnp.exp2(sk - m)
        ot = jnp.concatenate([acc[:HEAD_DIM, :tq] / den[:, :tq], acc[HEAD_DIM:LANES, tq:] / den[:, tq:]], axis=0)
        o_ref[0, sub * tq:(sub + 1) * tq, c * LANES:(c + 1) * LANES] = ot.T.astype(BF16)


def _window_call(sink, q, k, vt, tq, tiles):
    b, seq, _ = q.shape
    kern = functools.partial(_window_kernel, tq=tq, tiles=tiles, seq=seq, radius=A_RADIUS)
    return pl.pallas_call(
        kern, grid=(b, seq // (tq * tiles)),
        in_specs=[pl.BlockSpec(memory_space=pltpu.SMEM),
                  pl.BlockSpec((1, tq * tiles, A_Q_COLS), lambda i, n: (i, n, 0)),
                  pl.BlockSpec((1, seq, A_KV_COLS), lambda i, n: (i, 0, 0)),
                  pl.BlockSpec((A_KV_COLS, seq), lambda i, n: (0, i))],
        out_specs=pl.BlockSpec((1, tq * tiles, A_Q_COLS), lambda i, n: (i, n, 0)),
        out_shape=jax.ShapeDtypeStruct((b, seq, A_Q_COLS), BF16),
        compiler_params=pltpu.CompilerParams(dimension_semantics=("arbitrary", "arbitrary")),
        name="mixer_a",
    )(sink, q, k, vt)


def _dilated_kernel(q_ref, k_ref, v_ref, o_ref, lse_ref, *, dil, tq, seq, radius):
    sub_len = seq // dil
    n_tiles = sub_len // tq
    win = tq + 2 * radius

    def rows(first, count):
        if dil == 1:
            return pl.ds(pl.multiple_of(first, radius), count)
        return pl.ds(first, count, stride=dil)

    left = lax.broadcasted_iota(jnp.int32, (tq, LANES), 1) < HEAD_DIM
    ones = jnp.ones((win, LANES), BF16)

    blocks_per_iter = 8
    shared_window = win == sub_len and blocks_per_iter % n_tiles == 0

    def load_kv(r, start):
        kv_rows = rows(r + dil * start, win)
        return k_ref[0, kv_rows, :].astype(BF16), v_ref[0, kv_rows, :].astype(BF16)

    def scores(it, kv=None):
        r = it // n_tiles
        n = it - r * n_tiles
        start = jnp.clip(n * tq - radius, 0, sub_len - win)
        q_rows = rows(r + dil * (n * tq), tq)
        q = q_ref[0, q_rows, :].astype(BF16)
        k, v = load_kv(r, start) if kv is None else kv
        zero = jnp.zeros_like(q)
        q2 = jnp.concatenate([jnp.where(left, q, zero), jnp.where(left, zero, q)], axis=0)
        bias = _band_bias(n, tq, win, radius, start)
        return q_rows, v, _nt_dot(q2, k) + jnp.concatenate([bias, bias], axis=0)

    def finish(q_rows, v, s):
        m = jnp.max(s, axis=-1, keepdims=True)
        p = jnp.exp(s - m).astype(BF16)
        acc = _dot(p, jnp.concatenate([v, ones], axis=1))
        den = acc[:, LANES:]
        o2 = acc[:, :LANES] / den
        lse2 = jnp.log(den) + m
        o_ref[0, q_rows, :] = jnp.where(left, o2[:tq], o2[tq:])
        lse_ref[0, q_rows, :] = jnp.where(left, lse2[:tq], lse2[tq:])

    def body(j, carry):
        pending = []
        for u in range(blocks_per_iter):
            it = blocks_per_iter * j + u
            if shared_window and u % n_tiles == 0:
                kv = load_kv(it // n_tiles, 0)
            pending.append(scores(it, kv if shared_window else None))
        for blk in pending:
            finish(*blk)
        return carry

    lax.fori_loop(0, dil * n_tiles // blocks_per_iter, body, 0)


def _dilated_groups_kernel(q_ref, k_ref, v_ref, o_ref, lse_ref, *, tq, seq):
    group = pl.program_id(1)
    for gi, (window, dil) in enumerate(B_PAIRS):
        @pl.when(group == gi)
        def _(dil=dil, radius=window // (2 * dil)):
            _dilated_kernel(q_ref, k_ref, v_ref, o_ref, lse_ref, dil=dil, tq=tq, seq=seq, radius=radius)


def _dilated_call(q, k, v, tq):
    b, seq, _ = q.shape
    groups = len(B_PAIRS)
    kern = functools.partial(_dilated_groups_kernel, tq=tq, seq=seq)
    out_sds = jax.ShapeDtypeStruct((groups * b, seq, LANES), F32)
    chunk = lambda i, g: (i, 0, g)
    slab = lambda i, g: (g * b + i, 0, 0)
    o, lse = pl.pallas_call(
        kern, grid=(b, groups),
        in_specs=[pl.BlockSpec((1, seq, LANES), chunk)] * 3,
        out_specs=[pl.BlockSpec((1, seq, LANES), slab)] * 2,
        out_shape=[out_sds, out_sds],
        compiler_params=pltpu.CompilerParams(dimension_semantics=("arbitrary", "arbitrary")),
        name="mixer_b",
    )(q, k, v)
    return o.reshape(groups, b * seq, LANES), lse.reshape(groups, b * seq, LANES)


def _diff_kernel(q_ref, k_ref, vt_ref, lam_ref, subg_ref, o_ref, *, tq, tiles, tk, seq, lambda_init, lagged):
    nseg = LANES // C_QK_DIM
    lane = lax.broadcasted_iota(jnp.int32, (tq, LANES), 1)

    def masked_rows(sub):
        q = q_ref[0, sub * tq:(sub + 1) * tq, :]
        zero = jnp.zeros_like(q)
        return jnp.concatenate([jnp.where((lane // C_QK_DIM) == j, q, zero) for j in range(nseg)], axis=0)

    q4 = [masked_rows(sub) for sub in range(tiles)]
    ones_rows = jnp.ones((BF16_ROWS, tk), BF16)
    m = [jnp.full((1, nseg * tq), NEG_INF, F32)] * tiles
    ref = list(m)
    acc = [jnp.zeros((LANES + BF16_ROWS, nseg * tq), F32)] * tiles
    n_blocks = seq // tk
    jobs = [(t, sub) for t in range(n_blocks) for sub in range(tiles)]
    scores = lambda job: _nt_dot(k_ref[0, job[0] * tk:(job[0] + 1) * tk, :], q4[job[1]])
    ahead = (1 if lagged else 2) * tiles
    pending = [scores(job) for job in jobs[:ahead]]
    for j, (t, sub) in enumerate(jobs):
        st = pending.pop(0)
        if j + ahead < len(jobs):
            pending.append(scores(jobs[j + ahead]))
        colmax = jnp.max(st, axis=0, keepdims=True)
        ref_new = m[sub] if (lagged and t > 0) else jnp.maximum(m[sub], colmax)
        alpha = jnp.exp2(ref[sub] - ref_new)
        pt = jnp.exp2(st - ref_new).astype(BF16)
        vext = jnp.concatenate([vt_ref[:, t * tk:(t + 1) * tk], ones_rows], axis=0)
        acc[sub] = acc[sub] * alpha + _dot(vext, pt)
        ref[sub] = ref_new
        m[sub] = jnp.maximum(m[sub], colmax)

    lp = lam_ref[...]
    lam = (jnp.exp(jnp.sum(lp[0:1] * lp[1:2], axis=-1, keepdims=True))
           - jnp.exp(jnp.sum(lp[2:3] * lp[3:4], axis=-1, keepdims=True)) + lambda_init)
    for sub in range(tiles):
        heads = []
        for hh in range(LANES // C_V_DIM):
            u = []
            for c in range(2):
                cols = slice((2 * hh + c) * tq, (2 * hh + c + 1) * tq)
                u.append(acc[sub][hh * C_V_DIM:(hh + 1) * C_V_DIM, cols] / acc[sub][LANES:LANES + 1, cols])
            o = u[0] - lam * u[1]
            ss = jnp.sum(o * o, axis=0, keepdims=True)
            heads.append(o * lax.rsqrt(ss * (1.0 / C_V_DIM) + NORM_EPS))
        ot = jnp.concatenate(heads, axis=0) * subg_ref[...] * (1.0 - lambda_init)
        o_ref[0, sub * tq:(sub + 1) * tq, :] = ot.T.astype(BF16)


def _diff_call(q, k, vt, lam_p, subg, lambda_init, tq, tiles, tk, lagged):
    b, seq, _ = q.shape
    pairs = C_QK_COLS // LANES
    kern = functools.partial(_diff_kernel, tq=tq, tiles=tiles, tk=tk, seq=seq, lambda_init=lambda_init,
                             lagged=lagged)
    return pl.pallas_call(
        kern, grid=(b, pairs, seq // (tq * tiles)),
        in_specs=[pl.BlockSpec((1, tq * tiles, LANES), lambda i, p, n: (i, n, p)),
                  pl.BlockSpec((1, seq, LANES), lambda i, p, n: (i, 0, p)),
                  pl.BlockSpec((LANES, seq), lambda i, p, n: (p, i)),
                  pl.BlockSpec((4, C_QK_DIM), lambda i, p, n: (0, 0)),
                  pl.BlockSpec((LANES, tq), lambda i, p, n: (0, 0))],
        out_specs=pl.BlockSpec((1, tq * tiles, LANES), lambda i, p, n: (i, n, p)),
        out_shape=jax.ShapeDtypeStruct((b, seq, C_V_COLS), BF16),
        compiler_params=pltpu.CompilerParams(dimension_semantics=("arbitrary",) * 3),
        name="mixer_c_lagged" if lagged else "mixer_c",
    )(q, k, vt, lam_p, subg)


def _nbr_kernel(q_ref, k_ref, v_ref, bias_ref, o_ref, *, rows):
    kr = NA_ROWS
    lane = lax.broadcasted_iota(jnp.int32, (GRID_W, LANES), 1)
    left = lane < HEAD_DIM

    ones = jnp.ones((kr * GRID_W, LANES), BF16)

    def scores(i):
        r0 = jnp.clip(i - kr // 2, 0, rows - kr)
        cls = jnp.minimum(i, kr // 2) + jnp.maximum(i - (rows - kr // 2), 0)
        q = q_ref[0, pl.ds(pl.multiple_of(i * GRID_W, GRID_W), GRID_W), :]
        koff = pl.multiple_of(r0 * GRID_W, GRID_W)
        k = k_ref[0, pl.ds(koff, kr * GRID_W), :]
        v = v_ref[0, pl.ds(koff, kr * GRID_W), :]
        zero = jnp.zeros_like(q)
        q2 = jnp.concatenate([jnp.where(left, q, zero), jnp.where(left, zero, q)], axis=0)
        bias = jnp.concatenate(
            [jnp.concatenate([bias_ref[hh, NA_ROWS - 1 - cls + 2 * u] for u in range(kr // 2)], axis=1)
             for hh in range(2)], axis=0)
        return i, v, _nt_dot(q2, k) + bias

    def finish(i, v, s):
        m = jnp.max(s, axis=-1, keepdims=True)
        p = jnp.exp(s - m).astype(BF16)
        acc = _dot(p, jnp.concatenate([v, ones], axis=1))
        pv = acc[:, :LANES] / acc[:, LANES:]
        o = jnp.where(left, pv[:GRID_W], pv[GRID_W:])
        o_ref[0, pl.ds(pl.multiple_of(i * GRID_W, GRID_W), GRID_W), :] = o.astype(BF16)

    rows_per_iter = 16

    def body(j, carry):
        pending = [scores(rows_per_iter * j + u) for u in range(rows_per_iter)]
        for blk in pending:
            finish(*blk)
        return carry

    lax.fori_loop(0, rows // rows_per_iter, body, 0)


def _nbr_call(q, k, v, bias_tab):
    b, seq, _ = q.shape
    rows = seq // GRID_W
    pairs = D_COLS // LANES
    kern = functools.partial(_nbr_kernel, rows=rows)
    blk = lambda i, p: (i, 0, p)
    return pl.pallas_call(
        kern, grid=(b, pairs),
        in_specs=[pl.BlockSpec((1, seq, LANES), blk), pl.BlockSpec((1, seq, LANES), blk),
                  pl.BlockSpec((1, seq, LANES), blk),
                  pl.BlockSpec((2,) + bias_tab.shape[1:], lambda i, p: (p, 0, 0, 0))],
        out_specs=pl.BlockSpec((1, seq, LANES), blk),
        out_shape=jax.ShapeDtypeStruct((b, seq, D_COLS), BF16),
        compiler_params=pltpu.CompilerParams(dimension_semantics=("arbitrary",) * 2),
        name="mixer_d",
    )(q, k, v, bias_tab)


def _nbr_bias_table(rpb):
    cj = np.arange(GRID_W)
    c0 = np.clip(cj - NA_COLS // 2, 0, GRID_W - NA_COLS)
    col_ok = (cj[None, :] >= c0[:, None]) & (cj[None, :] < c0[:, None] + NA_COLS)
    dc = np.clip(cj[None, :] - cj[:, None], -(NA_COLS - 1), NA_COLS - 1) + (NA_COLS - 1)
    onehot = jnp.asarray(np.arange(2 * NA_COLS - 1)[:, None, None] == dc[None], dtype=F32)
    per_row = jnp.einsum('hrj,jqk->hrqk', rpb.astype(F32), onehot, precision=lax.Precision.HIGHEST)
    per_row = jnp.where(col_ok[None, None], per_row, NEG_INF)
    return jnp.concatenate([per_row[:, :-1], per_row[:, 1:]], axis=-1)


def _merge_mlp_kernel(x_ref, ya_ref, ob1_ref, ob2_ref, ob3_ref, lb1_ref, lb2_ref, lb3_ref, yc_ref, yd_ref,
                      ga_ref, gm_ref, wg_ref, wa_ref, wb_ref, wc_ref, wd_ref, wo_ref, wu_ref, wdn_ref, o_ref,
                      *, rows, tiles):
    subs = [slice(s * rows, (s + 1) * rows) for s in range(tiles)]
    weights = (wa_ref, wb_ref, wc_ref, wd_ref)
    xs, merged = [], []
    for sl in subs:
        x = x_ref[sl, :]
        h = _rms_rows(x, ga_ref[...]).astype(BF16)
        l1, l2, l3 = lb1_ref[sl, :], lb2_ref[sl, :], lb3_ref[sl, :]
        m = jnp.maximum(jnp.maximum(l1, l2), l3)
        w1, w2, w3 = jnp.exp(l1 - m), jnp.exp(l2 - m), jnp.exp(l3 - m)
        yb = (w1 * ob1_ref[sl, :] + w2 * ob2_ref[sl, :] + w3 * ob3_ref[sl, :]) / (w1 + w2 + w3)
        branches = (ya_ref[sl, :], yb.astype(BF16), yc_ref[sl, :], yd_ref[sl, :])
        total = None
        for c in range(N_BRANCHES):
            z = _dot(h, wg_ref[:, c * D_MODEL:(c + 1) * D_MODEL])
            term = _dot(branches[c], weights[c][...]) / (1.0 + jnp.exp(-z))
            total = term if total is None else total + term
        xs.append(x)
        merged.append(total)
    x1 = [x + _dot(t.astype(BF16), wo_ref[...]) for x, t in zip(xs, merged)]
    hm = [_rms_rows(v, gm_ref[...]).astype(BF16) for v in x1]
    acc = list(x1)
    n_chunks = D_MLP // D_MODEL
    jobs = [(c, s) for c in range(n_chunks) for s in range(tiles)]
    up = lambda job: _dot(hm[job[1]], wu_ref[:, job[0] * D_MODEL:(job[0] + 1) * D_MODEL])
    pending = [up(job) for job in jobs[:tiles]]
    for j, (c, s) in enumerate(jobs):
        u = pending.pop(0)
        if j + tiles < len(jobs):
            pending.append(up(jobs[j + tiles]))
        u = jnp.square(jnp.maximum(u, 0.0)).astype(BF16)
        acc[s] = acc[s] + _dot(u, wdn_ref[c * D_MODEL:(c + 1) * D_MODEL, :])
    for s, sl in enumerate(subs):
        o_ref[sl, :] = acc[s]


def _merge_mlp_call(x2d, ya, obs, lbs, yc, yd, ga, gm, wg, wa, wb, wc, wd, wo, wu, wdn, tm, tiles):
    t_tokens = x2d.shape[0]
    row = lambda i: (i, 0)
    const = lambda i: (0, 0)
    single = pl.Buffered(1)
    groups = obs.shape[0]
    acts = [x2d, ya, *([obs] * groups), *([lbs] * groups), yc, yd]
    consts = [ga, gm, wg, wa, wb, wc, wd, wo, wu, wdn]
    rows_spec = lambda a: pl.BlockSpec((tm * tiles, a.shape[1]), row)
    slab_spec = lambda g: pl.BlockSpec((None, tm * tiles, LANES), lambda i: (g, i, 0))
    in_specs = ([rows_spec(x2d), rows_spec(ya)] + [slab_spec(g) for g in range(groups)] * 2
                + [rows_spec(yc), rows_spec(yd)]
                + [pl.BlockSpec(w.shape, const, pipeline_mode=single) for w in consts])
    return pl.pallas_call(
        functools.partial(_merge_mlp_kernel, rows=tm, tiles=tiles), grid=(t_tokens // (tm * tiles),),
        in_specs=in_specs, out_specs=pl.BlockSpec((tm * tiles, D_MODEL), row),
        out_shape=jax.ShapeDtypeStruct((t_tokens, D_MODEL), F32),
        compiler_params=pltpu.CompilerParams(dimension_semantics=("arbitrary",), vmem_limit_bytes=VMEM_LIMIT),
        name="merge_mlp",
    )(*acts, *consts)


def _rope_lane_tables(seq, dim):
    inv = ROPE_THETA ** (-jnp.arange(0, dim, 2, dtype=F32) / dim)
    ang = jnp.arange(seq, dtype=F32)[:, None] * inv[None, :]
    cos, sin = jnp.cos(ang), jnp.sin(ang)
    reps = MXU_COLS // dim
    return (jnp.tile(jnp.concatenate([cos, cos], axis=-1), (1, reps)),
            jnp.tile(jnp.concatenate([-sin, sin], axis=-1), (1, reps)))


def _blockdiag_ones(seg):
    idx = np.arange(MXU_COLS) // seg
    return jnp.asarray(idx[:, None] == idx[None, :], dtype=BF16)


def _gain_row(l, a_g, b_g, c_g, d_g):
    ones = lambda n: jnp.ones((n,), F32)
    s64, s32 = HEAD_DIM ** -0.5, C_QK_DIM ** -0.5
    parts = [
        jnp.tile(a_g[l, 0] * (s64 * LOG2E), A_Q_HEADS), jnp.tile(a_g[l, 1], A_KV_HEADS), ones(A_KV_COLS),
        jnp.tile(b_g[l, 0] * s64, B_HEADS), jnp.tile(b_g[l, 1], B_HEADS), ones(B_COLS),
        jnp.tile(c_g[l, 0] * (s32 * LOG2E), 2 * C_HEADS), jnp.tile(c_g[l, 1], 2 * C_HEADS), ones(C_V_COLS),
        jnp.tile(d_g[l, 0] * s64, D_HEADS), jnp.tile(d_g[l, 1], D_HEADS), ones(D_COLS),
    ]
    return jnp.concatenate(parts).astype(F32)[None, :]


def kernel(x, attn_norm_g, w_in, a_qk_norm_g, a_sink, b_qk_norm_g, c_qk_norm_g, c_lambda, c_subln_g,
           d_qk_norm_g, d_rel_bias, w_branch_a, w_branch_b, w_branch_c, w_branch_d, w_out, mlp_norm_g,
           w_up, w_down):
    b, seq, _ = x.shape
    depth = w_in.shape[0]
    tokens = b * seq
    tm, tm_proj, tq_c = TILES["merge_rows"], TILES["proj_rows"], TILES["c_queries"]
    cos64, sin64 = _rope_lane_tables(seq, HEAD_DIM)
    cos32, sin32 = _rope_lane_tables(seq, C_QK_DIM)
    tabs = dict(cos64=cos64, sin64=sin64, cos32=cos32, sin32=sin32,
                e64=_blockdiag_ones(HEAD_DIM), e32=_blockdiag_ones(C_QK_DIM))
    xf = x.reshape(tokens, D_MODEL)
    for l in range(depth):
        w_l = w_in[l].astype(BF16)
        gains = _gain_row(l, a_qk_norm_g, b_qk_norm_g, c_qk_norm_g, d_qk_norm_g)
        w_aq = w_l[:, :A_Q_COLS].reshape(D_MODEL, A_Q_HEADS, HEAD_DIM)[:, A_HEAD_ORDER, :].reshape(D_MODEL, A_Q_COLS)
        wqkv = jnp.concatenate([w_aq, w_l[:, A_Q_COLS:QKV_COLS]], axis=1)
        wt = jnp.concatenate([w_l[:, _SECTION_COL[n]:_SECTION_COL[n] + dict((s[0], s[1]) for s in _SECTIONS)[n]].T
                              for n in _TRANSPOSED], axis=0)
        w_ba = (w_branch_a[l].astype(BF16).reshape(A_Q_HEADS, HEAD_DIM, D_MODEL)[A_HEAD_ORDER, :, :]
                .reshape(A_Q_COLS, D_MODEL))
        outs = _proj_call(xf, attn_norm_g[l][None, :], wqkv, wt, gains, tabs, seq, tm_proj,
                          TILES["proj_tiles_per_step"])
        aq, ak, avt, bq, bk, bv, cq, ck, cvt, dq, dk, dv = outs
        tok3 = lambda t: t.reshape(b, seq, t.shape[-1])
        ya = _window_call(a_sink[l].astype(F32), tok3(aq), tok3(ak), avt, tq=TILES["a_queries"],
                          tiles=TILES["a_tiles_per_step"])
        obs, lbs = _dilated_call(tok3(bq), tok3(bk), tok3(bv), tq=TILES["b_queries"])
        lambda_init = 0.8 - 0.6 * math.exp(-0.3 * l)
        subg = jnp.broadcast_to(jnp.tile(c_subln_g[l].astype(F32), LANES // C_V_DIM)[:, None], (LANES, tq_c))
        logit_bound = (C_QK_DIM ** 0.5 * LOG2E) * jnp.max(jnp.abs(c_qk_norm_g[l, 0])) * jnp.max(jnp.abs(c_qk_norm_g[l, 1]))
        c_args = (tok3(cq), tok3(ck), cvt, c_lambda[l].astype(F32), subg)
        yc = lax.cond(
            logit_bound < C_LAGGED_LOGIT_LIMIT,
            lambda *a: _diff_call(*a, lambda_init, tq=tq_c, tiles=TILES["c_tiles_per_step"], tk=TILES["c_keys"],
                                  lagged=True),
            lambda *a: _diff_call(*a, lambda_init, tq=tq_c, tiles=TILES["c_tiles_per_step_exact"],
                                  tk=TILES["c_keys_exact"], lagged=False),
            *c_args)
        yd = _nbr_call(tok3(dq), tok3(dk), tok3(dv), _nbr_bias_table(d_rel_bias[l]))
        xf = _merge_mlp_call(xf, ya.reshape(tokens, A_Q_COLS), obs, lbs, yc.reshape(tokens, C_V_COLS),
                             yd.reshape(tokens, D_COLS), attn_norm_g[l][None, :], mlp_norm_g[l][None, :],
                             w_l[:, QKV_COLS:], w_ba, w_branch_b[l].astype(BF16), w_branch_c[l].astype(BF16),
                             w_branch_d[l].astype(BF16), w_out[l].astype(BF16), w_up[l].astype(BF16),
                             w_down[l].astype(BF16), tm, TILES["merge_tiles_per_step"])
    return xf.reshape(b, seq, D_MODEL)
```
